```python
import jax, jax.numpy as jnp
from jax import lax
import numpy as np

D_MODEL = 1024
BATCH = 2
SEQ = 16384
DEPTH = 2

GRID_W = 64
CTX_LEN = 256
N_MIXERS = 2
N_HGRN_LAYERS = (DEPTH + N_MIXERS - 1) // N_MIXERS
N_CONV_LAYERS = DEPTH // N_MIXERS
HGRN_DK = 128
HGRN_HEADS = D_MODEL // HGRN_DK
HGRN_DV = D_MODEL // HGRN_HEADS
HGRN_CHUNK = 64
CONV_WIDTH = 31
N_EXPERTS = 32
TOP_K = 4
D_EXPERT = D_MODEL
SWIGLU_ALPHA = 1.702
SWIGLU_LIMIT = 7.0
MOE_BLOCK = 256
EPS = 1e-6

kernel_name = "hybrid_hgrn2_conformer_moe_dit"


def rmsnorm(x, g):
    x32 = x.astype(jnp.float32)
    y = x32 * lax.rsqrt(jnp.mean(x32 * x32, axis=-1, keepdims=True) + EPS)
    return (y * g.astype(jnp.float32)).astype(x.dtype)


def layernorm(x, g, b):
    x32 = x.astype(jnp.float32)
    mu = jnp.mean(x32, axis=-1, keepdims=True)
    xc = x32 - mu
    y = xc * lax.rsqrt(jnp.mean(xc * xc, axis=-1, keepdims=True) + EPS)
    return (y * g.astype(jnp.float32) + b.astype(jnp.float32)).astype(x.dtype)


def ada_mods(cond, w, b):
    return jnp.split(jax.nn.silu(cond) @ w + b, 6, axis=-1)


def gla_chunk_scan(q, k, v, logf, s0):
    bsz, nh, length, dk = q.shape
    dv = v.shape[-1]
    nc = length // HGRN_CHUNK

    def to_chunks(t):
        return jnp.moveaxis(t.reshape(bsz, nh, nc, HGRN_CHUNK, t.shape[-1]), 2, 0)

    mask = jnp.tril(jnp.ones((HGRN_CHUNK, HGRN_CHUNK), dtype=bool))[:, :, None]

    def step(s, inp):
        qc, kc, vc, gc = inp
        bcum = jnp.cumsum(gc, axis=-2)
        diff = bcum[..., :, None, :] - bcum[..., None, :, :]
        decay = jnp.exp(jnp.where(mask, diff, -jnp.inf))
        a = jnp.einsum('bhtk,bhsk,bhtsk->bhts', qc, kc, decay)
        o = (jnp.einsum('bhts,bhsv->bhtv', a, vc)
             + jnp.einsum('bhtk,bhkv->bhtv', qc * jnp.exp(bcum), s))
        blast = bcum[..., -1:, :]
        s_new = (jnp.exp(blast[..., 0, :])[..., None] * s
                 + jnp.einsum('bhsk,bhsv->bhkv', kc * jnp.exp(blast - bcum), vc))
        return s_new, o

    s_fin, o = lax.scan(step, s0, (to_chunks(q), to_chunks(k), to_chunks(v), to_chunks(logf)))
    o = jnp.moveaxis(o, 0, 2).reshape(bsz, nh, length, dv)
    return o, s_fin


def hgrn2_mixer(h_lat, h_ctx, w_in, lb_fwd, lb_bwd, g_out, w_out, return_ctx):
    hk = HGRN_HEADS * HGRN_DK
    hv = HGRN_HEADS * HGRN_DV
    splits = [hk, hk + hv, 2 * hk + hv, 3 * hk + hv]

    def heads(t):
        return t.reshape(t.shape[0], t.shape[1], HGRN_HEADS, -1).transpose(0, 2, 1, 3)

    def project(h):
        p = (h @ w_in).astype(jnp.float32)
        q, i, zf, zb, g = jnp.split(p, splits, axis=-1)
        f_f = lb_fwd + (1.0 - lb_fwd) * jax.nn.sigmoid(zf)
        f_b = lb_bwd + (1.0 - lb_bwd) * jax.nn.sigmoid(zb)
        return heads(q), heads(i), heads(f_f), heads(f_b), g

    def flip(t):
        return jnp.flip(t, axis=2)

    def scan_both(q, i, f_f, f_b, s_f, s_b):
        o_f, s_f_new = gla_chunk_scan(q, 1.0 - f_f, i, jnp.log(f_f), s_f)
        o_b, s_b_new = gla_chunk_scan(flip(q), flip(1.0 - f_b), flip(i), flip(jnp.log(f_b)), s_b)
        return o_f + flip(o_b), s_f_new, s_b_new

    def readout(o, g, dtype):
        o = o * lax.rsqrt(jnp.mean(o * o, axis=-1, keepdims=True) + EPS)
        o = o * g_out.astype(jnp.float32).reshape(HGRN_HEADS, HGRN_DV)[None, :, None, :]
        o = o.transpose(0, 2, 1, 3).reshape(o.shape[0], o.shape[2], hv)
        return (o * jax.nn.silu(g)).astype(dtype) @ w_out

    bsz = h_lat.shape[0]
    s0 = jnp.zeros((bsz, HGRN_HEADS, HGRN_DK, HGRN_DV), jnp.float32)
    qc, ic, ffc, fbc, gc = project(h_ctx)
    o_ctx, s_ctx_f, s_ctx_b = scan_both(qc, ic, ffc, fbc, s0, s0)
    ql, il, ffl, fbl, gl = project(h_lat)
    o_lat, _, _ = scan_both(ql, il, ffl, fbl, s_ctx_f, s_ctx_b)
    y_lat = readout(o_lat, gl, h_lat.dtype)
    y_ctx = readout(o_ctx, gc, h_ctx.dtype) if return_ctx else None
    return y_lat, y_ctx


def dwconv_latent(u, w_dw):
    bsz, length, d = u.shape
    rows = length // GRID_W
    grid = u.reshape(bsz, rows, GRID_W, d)
    half = d // 2
    dn = ('NHWC', 'HWIO', 'NHWC')
    horiz = lax.conv_general_dilated(grid[..., :half], w_dw[:, :half].reshape(1, CONV_WIDTH, 1, half),
                                     (1, 1), 'SAME', dimension_numbers=dn, feature_group_count=half)
    vert = lax.conv_general_dilated(grid[..., half:], w_dw[:, half:].reshape(CONV_WIDTH, 1, 1, d - half),
                                    (1, 1), 'SAME', dimension_numbers=dn, feature_group_count=d - half)
    return jnp.concatenate([horiz, vert], axis=-1).reshape(bsz, length, d)


def dwconv_seq(u, w_dw):
    d = u.shape[-1]
    return lax.conv_general_dilated(u, w_dw.reshape(CONV_WIDTH, 1, d), (1,), 'SAME',
                                    dimension_numbers=('NWC', 'WIO', 'NWC'), feature_group_count=d)


def conformer_conv(h, dwconv, w_in, b_in, w_dw, b_dw, g_ln, b_ln, w_out, b_out):
    a, gt = jnp.split(h @ w_in + b_in, 2, axis=-1)
    u = a * jax.nn.sigmoid(gt)
    z = dwconv(u, w_dw) + b_dw
    z = jax.nn.silu(layernorm(z, g_ln, b_ln))
    return z @ w_out + b_out


def moe_ffn(x, w_r, b_r, w1, b1, w2, b2):
    shp = x.shape
    xf = x.reshape(-1, shp[-1])
    n = xf.shape[0]
    nk = n * TOP_K
    logits = (xf @ w_r + b_r).astype(jnp.float32)
    top_val, top_idx = lax.top_k(logits, TOP_K)
    gates = jax.nn.softmax(top_val, axis=-1)
    e_flat = top_idx.reshape(-1)
    w_flat = gates.reshape(-1)
    tok_flat = jnp.arange(nk, dtype=jnp.int32) // TOP_K
    order = jnp.argsort(e_flat, stable=True)
    e_sorted = e_flat[order]
    sizes = jnp.bincount(e_flat, length=N_EXPERTS)
    group_start = jnp.cumsum(sizes) - sizes
    padded = (sizes + MOE_BLOCK - 1) // MOE_BLOCK * MOE_BLOCK
    padded_end = jnp.cumsum(padded)
    padded_start = padded_end - padded
    dest = padded_start[e_sorted] + jnp.arange(nk, dtype=jnp.int32) - group_start[e_sorted]
    n_blocks = -(-(nk + N_EXPERTS * (MOE_BLOCK - 1)) // MOE_BLOCK)
    cap = n_blocks * MOE_BLOCK
    buf_tok = jnp.zeros((cap,), jnp.int32).at[dest].set(tok_flat[order])
    buf_w = jnp.zeros((cap,), jnp.float32).at[dest].set(w_flat[order])
    block_e = jnp.minimum(jnp.searchsorted(padded_end, jnp.arange(n_blocks) * MOE_BLOCK, side='right'),
                          N_EXPERTS - 1)

    def expert_block(args):
        tok_b, e = args
        xb = xf[tok_b]
        gate, up = jnp.split(xb @ w1[e] + b1[e], 2, axis=-1)
        gate = jnp.minimum(gate, SWIGLU_LIMIT)
        up = jnp.clip(up, -SWIGLU_LIMIT, SWIGLU_LIMIT)
        act = (up + 1.0) * gate * jax.nn.sigmoid(SWIGLU_ALPHA * gate)
        return act @ w2[e] + b2[e]

    out = lax.map(expert_block, (buf_tok.reshape(n_blocks, MOE_BLOCK), block_e))
    out = out.reshape(cap, shp[-1]) * buf_w[:, None].astype(out.dtype)
    y = jax.ops.segment_sum(out, buf_tok, num_segments=n)
    return y.reshape(shp)


def setup_inputs(seed: int = 0) -> dict:
    key = jax.random.key(seed)
    ks = iter(jax.random.split(key, 32))
    d = D_MODEL
    hk = HGRN_HEADS * HGRN_DK
    hv = HGRN_HEADS * HGRN_DV
    nh, ncv = N_HGRN_LAYERS, N_CONV_LAYERS

    def nrm(shape, scale):
        return jax.random.normal(next(ks), shape, jnp.float32) * scale

    def gain(shape):
        return 1.0 + nrm(shape, 0.02)

    return {
        "x": nrm((BATCH, SEQ, d), 1.0),
        "c": nrm((BATCH, d), 1.0),
        "ctx": nrm((BATCH, CTX_LEN, d), 1.0),
        "c_ctx": nrm((d,), 1.0),
        "w_ada": nrm((DEPTH, d, 6 * d), 0.5 * d ** -0.5),
        "b_ada": nrm((DEPTH, 6 * d), 0.02),
        "g_mix": gain((DEPTH, d)),
        "g_ffn": gain((DEPTH, d)),
        "w_hgrn_in": nrm((nh, d, 3 * hk + 2 * hv), d ** -0.5),
        "hgrn_gamma": nrm((2, nh + 1, hk), 0.5),
        "g_hgrn_out": gain((nh, hv)),
        "w_hgrn_out": nrm((nh, hv, d), hv ** -0.5),
        "w_cv_in": nrm((ncv, d, 2 * d), d ** -0.5),
        "b_cv_in": nrm((ncv, 2 * d), 0.02),
        "w_cv_dw": nrm((ncv, CONV_WIDTH, d), CONV_WIDTH ** -0.5),
        "b_cv_dw": nrm((ncv, d), 0.02),
        "g_cv_ln": gain((ncv, d)),
        "b_cv_ln": nrm((ncv, d), 0.02),
        "w_cv_out": nrm((ncv, d, d), d ** -0.5),
        "b_cv_out": nrm((ncv, d), 0.02),
        "w_router": nrm((DEPTH, d, N_EXPERTS), d ** -0.5),
        "b_router": nrm((DEPTH, N_EXPERTS), 0.01),
        "w_exp_in": nrm((DEPTH, N_EXPERTS, d, 2 * D_EXPERT), d ** -0.5),
        "b_exp_in": nrm((DEPTH, N_EXPERTS, 2 * D_EXPERT), 0.01),
        "w_exp_out": nrm((DEPTH, N_EXPERTS, D_EXPERT, d), D_EXPERT ** -0.5),
        "b_exp_out": nrm((DEPTH, N_EXPERTS, d), 0.01),
        "g_final": gain((d,)),
    }


def reference(x, c, ctx, c_ctx, w_ada, b_ada, g_mix, g_ffn, w_hgrn_in, hgrn_gamma, g_hgrn_out,
              w_hgrn_out, w_cv_in, b_cv_in, w_cv_dw, b_cv_dw, g_cv_ln, b_cv_ln, w_cv_out, b_cv_out,
              w_router, b_router, w_exp_in, b_exp_in, w_exp_out, b_exp_out, g_final):
    lb_all = jnp.cumsum(jax.nn.softmax(hgrn_gamma.astype(jnp.float32), axis=1), axis=1)
    ctx_s = ctx
    for l in range(DEPTH):
        mixer = l % N_MIXERS
        j = l // N_MIXERS
        ctx_later = any(m % N_MIXERS == 0 for m in range(l + 1, DEPTH))
        sh_m, sc_m, gt_m, sh_f, sc_f, gt_f = [m[:, None, :] for m in ada_mods(c, w_ada[l], b_ada[l])]
        h_lat = rmsnorm(x, g_mix[l]) * (1.0 + sc_m) + sh_m
        if mixer == 0 or ctx_later:
            csh_m, csc_m, cgt_m, csh_f, csc_f, cgt_f = ada_mods(c_ctx, w_ada[l], b_ada[l])
            h_ctx = rmsnorm(ctx_s, g_mix[l]) * (1.0 + csc_m) + csh_m
        if mixer == 0:
            y_lat, y_ctx = hgrn2_mixer(h_lat, h_ctx, w_hgrn_in[j], lb_all[0, j], lb_all[1, j],
                                       g_hgrn_out[j], w_hgrn_out[j], ctx_later)
        else:
            conv_p = (w_cv_in[j], b_cv_in[j], w_cv_dw[j], b_cv_dw[j], g_cv_ln[j], b_cv_ln[j],
                      w_cv_out[j], b_cv_out[j])
            y_lat = conformer_conv(h_lat, dwconv_latent, *conv_p)
            y_ctx = conformer_conv(h_ctx, dwconv_seq, *conv_p) if ctx_later else None
        x = x + gt_m * y_lat
        hf_lat = rmsnorm(x, g_ffn[l]) * (1.0 + sc_f) + sh_f
        moe_p = (w_router[l], b_router[l], w_exp_in[l], b_exp_in[l], w_exp_out[l], b_exp_out[l])
        if ctx_later:
            ctx_s = ctx_s + cgt_m * y_ctx
            hf_ctx = rmsnorm(ctx_s, g_ffn[l]) * (1.0 + csc_f) + csh_f
            n_lat = hf_lat.shape[0] * hf_lat.shape[1]
            y_all = moe_ffn(jnp.concatenate([hf_lat.reshape(-1, D_MODEL), hf_ctx.reshape(-1, D_MODEL)], axis=0),
                            *moe_p)
            x = x + gt_f * y_all[:n_lat].reshape(x.shape)
            ctx_s = ctx_s + cgt_f * y_all[n_lat:].reshape(ctx_s.shape)
        else:
            x = x + gt_f * moe_ffn(hf_lat, *moe_p)
    return rmsnorm(x, g_final)
```

```python
import functools

import jax
import jax.numpy as jnp
from jax import lax
from jax.experimental import pallas as pl
from jax.experimental.pallas import tpu as pltpu

F32 = jnp.float32
BF16 = jnp.bfloat16
I32 = jnp.int32
U32 = jnp.uint32

EPS = 1e-6
HEAD_DIM = 128
GLA_CHUNK = 64
GLA_DECAY_LIMIT = 60.0
GRID_W = 64
CONV_WIDTH = 31
TOP_K = 4
SWIGLU_ALPHA = 1.702
SWIGLU_LIMIT = 7.0

V7X_LANES = 128
V7X_VMEM_BYTES = 64 * 1024 * 1024
VMEM_HEADROOM_BYTES = 8 * 1024 * 1024


def _params(semantics, vmem_bytes):
    limit = min(int(vmem_bytes), V7X_VMEM_BYTES - VMEM_HEADROOM_BYTES)
    return pltpu.CompilerParams(dimension_semantics=semantics, vmem_limit_bytes=limit)


def _dot(a, b):
    return jnp.dot(a, b, preferred_element_type=F32)


def _dot_nt(a, b):
    return lax.dot_general(a, b, (((1,), (1,)), ((), ())), preferred_element_type=F32)


def _dot_tn(a, b):
    return lax.dot_general(a, b, (((0,), (0,)), ((), ())), preferred_element_type=F32)


def _rmsnorm_mod(x, g, scale, shift):
    y = x * lax.rsqrt(jnp.mean(x * x, axis=-1, keepdims=True) + EPS)
    return y * g * (1.0 + scale) + shift


def _store_row_slabs(ref, x):
    for s in range(ref.shape[-2]):
        ref[:, s, :] = x[:, s * V7X_LANES:(s + 1) * V7X_LANES].astype(BF16)


def _load_row_slabs(ref):
    return jnp.concatenate([ref[:, s, :] for s in range(ref.shape[-2])], axis=1)


def _ada_kernel(c_ref, w_ref, b_ref, o_ref):
    s = c_ref[...]
    s = s * jax.nn.sigmoid(s)
    o_ref[0] = jnp.dot(s, w_ref[0], preferred_element_type=F32,
                       precision=lax.Precision.HIGHEST) + b_ref[0]


def _ada_mods(cond, w_ada, b_ada):
    depth, d, d6 = w_ada.shape
    rows = cond.shape[0]
    tn = 1536
    return pl.pallas_call(
        _ada_kernel,
        grid=(depth, d6 // tn),
        in_specs=[pl.BlockSpec((rows, d), lambda l, n: (0, 0)),
                  pl.BlockSpec((1, d, tn), lambda l, n: (l, 0, n)),
                  pl.BlockSpec((1, 1, tn), lambda l, n: (l, 0, n))],
        out_specs=pl.BlockSpec((1, rows, tn), lambda l, n: (l, 0, n)),
        out_shape=jax.ShapeDtypeStruct((depth, rows, d6), F32),
        compiler_params=_params(("arbitrary", "arbitrary"), 24 << 20),
        name="ada_mods",
    )(cond, w_ada, b_ada.reshape(depth, 1, d6))


def _chunk_cumsum(x, chunk, reverse):
    rows, cols = x.shape
    pos = lax.broadcasted_iota(I32, (rows, V7X_LANES), 0) & (chunk - 1)
    step = 1
    while step < chunk:
        keep = (pos < chunk - step) if reverse else (pos >= step)
        shift = rows - step if reverse else step
        parts = []
        for c in range(cols // V7X_LANES):
            xs = x[:, c * V7X_LANES:(c + 1) * V7X_LANES]
            parts.append(xs + jnp.where(keep, pltpu.roll(xs, shift, 0), 0.0))
        x = jnp.concatenate(parts, axis=1)
        step *= 2
    return x


def _hgrn_proj_kernel(x_ref, mod_ref, gmix_ref, lb_ref, w_ref,
                      q_ref, kf_ref, kb_ref, v_ref, bf_ref, cb_ref, g_ref):
    d = x_ref.shape[-1]
    h = _rmsnorm_mod(x_ref[0], gmix_ref[...], mod_ref[0, 1:2, :], mod_ref[0, 0:1, :])
    p = _dot(h.astype(BF16), w_ref[...])
    q_ref[0] = p[:, 0:d].astype(BF16)
    v_ref[0] = p[:, d:2 * d].astype(BF16)
    g_ref[0] = p[:, 4 * d:5 * d].astype(BF16)
    lbf = lb_ref[0:1, :]
    lbb = lb_ref[1:2, :]
    ff = lbf + (1.0 - lbf) * jax.nn.sigmoid(p[:, 2 * d:3 * d])
    fb = lbb + (1.0 - lbb) * jax.nn.sigmoid(p[:, 3 * d:4 * d])
    kf_ref[0] = (1.0 - ff).astype(BF16)
    kb_ref[0] = (1.0 - fb).astype(BF16)
    bf_ref[0] = _chunk_cumsum(jnp.log(ff), GLA_CHUNK, reverse=False)
    cb_ref[0] = _chunk_cumsum(jnp.log(fb), GLA_CHUNK, reverse=True)


def _hgrn_proj(x, mods, g_mix, lb, w_in, tm):
    b, l, d = x.shape
    d5 = w_in.shape[1]
    tok = lambda bi, i: (bi, i, 0)
    const2 = lambda bi, i: (0, 0)
    bf_sd = jax.ShapeDtypeStruct((b, l, d), BF16)
    f_sd = jax.ShapeDtypeStruct((b, l, d), F32)
    return pl.pallas_call(
        _hgrn_proj_kernel,
        grid=(b, l // tm),
        in_specs=[pl.BlockSpec((1, tm, d), tok),
                  pl.BlockSpec((1, 6, d), lambda bi, i: (bi, 0, 0)),
                  pl.BlockSpec((1, d), const2),
                  pl.BlockSpec((2, d), const2),
                  pl.BlockSpec((d, d5), const2)],
        out_specs=[pl.BlockSpec((1, tm, d), tok)] * 7,
        out_shape=[bf_sd, bf_sd, bf_sd, bf_sd, f_sd, f_sd, bf_sd],
        compiler_params=_params(("arbitrary", "arbitrary"), 52 << 20),
        name="hgrn_proj",
    )(x, mods, g_mix, lb, w_in)


def _gla_chunk(q, k, v, cum, st, *, reverse, stable):
    c = q.shape[0]
    qf = q.astype(F32)
    kf = k.astype(F32)
    last = cum[0:1] if reverse else cum[c - 1:c]
    row = lax.broadcasted_iota(I32, (c, c), 0)
    col = lax.broadcasted_iota(I32, (c, c), 1)
    if not stable:
        mid_row = c // 2 if reverse else c // 2 - 1
        mid = cum[mid_row:mid_row + 1]
        x = cum - mid
        qe = qf * jnp.exp(x)
        ke = kf * jnp.exp(-x)
        a = _dot_nt(qe.astype(BF16), ke.astype(BF16))
        a = jnp.where((col >= row) if reverse else (row >= col), a, 0.0)
        qh = (qe * jnp.exp(mid)).astype(BF16)
        kh = (ke * jnp.exp(last - mid)).astype(BF16)
    else:
        a = jnp.where(row == col, _dot_nt(q, k), 0.0)
        c1 = cum.astype(BF16)
        r1 = cum - c1.astype(F32)
        c2 = r1.astype(BF16)
        c3 = (r1 - c2.astype(F32)).astype(BF16)
        level = 0
        while (1 << level) < c:
            m = 1 << level
            seg_row = row >> (level + 1)
            ref_row = (seg_row << (level + 1)) + (m if reverse else m - 1)
            sel = jnp.where(col == ref_row, 1.0, 0.0).astype(BF16)
            cref = _dot(sel, c1) + _dot(sel, c2) + _dot(sel, c3)
            e = jnp.exp(-jnp.abs(cum - cref))
            contrib = _dot_nt((qf * e).astype(BF16), (kf * e).astype(BF16))
            row_bit = (row >> level) & 1
            col_bit = (col >> level) & 1
            same = seg_row == (col >> (level + 1))
            if reverse:
                pick = jnp.where(same, (1 - row_bit) * col_bit, 0)
            else:
                pick = jnp.where(same, row_bit * (1 - col_bit), 0)
            a = a + jnp.where(pick == 1, contrib, 0.0)
            level += 1
        qh = (qf * jnp.exp(cum)).astype(BF16)
        kh = (kf * jnp.exp(last - cum)).astype(BF16)
    o = _dot(a.astype(BF16), v) + _dot_nt(qh, st.astype(BF16))
    st_new = st * jnp.exp(last) + _dot_tn(v, kh)
    return o, st_new


def _gla_scan_kernel(qf_ref, kf_ref, vf_ref, bf_ref, qb_ref, kb_ref, vb_ref, cb_ref, s0f_ref, s0b_ref,
                     of_ref, ob_ref, sf_ref, sb_ref):
    j = pl.program_id(2)

    @pl.when(j == 0)
    def _():
        sf_ref[...] = s0f_ref[...]
        sb_ref[...] = s0b_ref[...]

    t = qf_ref.shape[1]
    n_chunks = t // GLA_CHUNK
    n_heads = qf_ref.shape[2] // HEAD_DIM
    half = GLA_CHUNK // 2

    worst = jnp.zeros((1, qf_ref.shape[2]), F32)
    for c in range(n_chunks):
        lo = c * GLA_CHUNK
        bm = bf_ref[0, lo + half - 1:lo + half, :]
        bl = bf_ref[0, lo + GLA_CHUNK - 1:lo + GLA_CHUNK, :]
        cm = cb_ref[0, lo + half:lo + half + 1, :]
        c0 = cb_ref[0, lo:lo + 1, :]
        worst = jnp.maximum(jnp.maximum(worst, jnp.maximum(-bm, bm - bl)), jnp.maximum(-cm, cm - c0))
    steep = jnp.max(worst) > GLA_DECAY_LIMIT

    def run(stable):
        for h in range(n_heads):
            ls = slice(h * HEAD_DIM, (h + 1) * HEAD_DIM)
            st = sf_ref[0, h]
            for c in range(n_chunks):
                rs = slice(c * GLA_CHUNK, (c + 1) * GLA_CHUNK)
                o, st = _gla_chunk(qf_ref[0, rs, ls], kf_ref[0, rs, ls], vf_ref[0, rs, ls], bf_ref[0, rs, ls],
                                   st, reverse=False, stable=stable)
                of_ref[0, rs, ls] = o
            sf_ref[0, h] = st
            st = sb_ref[0, h]
            for c in reversed(range(n_chunks)):
                rs = slice(c * GLA_CHUNK, (c + 1) * GLA_CHUNK)
                o, st = _gla_chunk(qb_ref[0, rs, ls], kb_ref[0, rs, ls], vb_ref[0, rs, ls], cb_ref[0, rs, ls],
                                   st, reverse=True, stable=stable)
                ob_ref[0, rs, ls] = o
            sb_ref[0, h] = st

    @pl.when(steep)
    def _():
        run(True)

    @pl.when(jnp.logical_not(steep))
    def _():
        run(False)


def _gla_scan(q, kf, kb, v, bf, cb, s0f, s0b, t, heads_per_step):
    b, l, d = q.shape
    lanes = heads_per_step * HEAD_DIM
    nblk = l // t
    fwd = lambda bi, hp, j: (bi, j, hp)
    bwd = lambda bi, hp, j: (bi, nblk - 1 - j, hp)
    st_map = lambda bi, hp, j: (bi, hp, 0, 0)
    tok_f = pl.BlockSpec((1, t, lanes), fwd)
    tok_b = pl.BlockSpec((1, t, lanes), bwd)
    st_spec = pl.BlockSpec((1, heads_per_step, HEAD_DIM, HEAD_DIM), st_map)
    o_sd = jax.ShapeDtypeStruct((b, l, d), F32)
    s_sd = jax.ShapeDtypeStruct(s0f.shape, F32)
    return pl.pallas_call(
        _gla_scan_kernel,
        grid=(b, d // lanes, nblk),
        in_specs=[tok_f, tok_f, tok_f, tok_f, tok_b, tok_b, tok_b, tok_b, st_spec, st_spec],
        out_specs=[tok_f, tok_b, st_spec, st_spec],
        out_shape=[o_sd, o_sd, s_sd, s_sd],
        compiler_params=_params(("arbitrary", "arbitrary", "arbitrary"), 32 << 20),
        name="gla_scan",
    )(q, kf, v, bf, q, kb, v, cb, s0f, s0b)


def _ffn_route_tail(x, y, mod_ref, gffn_ref, wr_ref, br_ref, x1_ref, hfp_ref, route_ref, gate_ref, cnt_ref):
    tm = x.shape[0]
    x1 = x + mod_ref[0, 2:3, :] * y
    x1_ref[...] = x1
    hf = _rmsnorm_mod(x1, gffn_ref[...], mod_ref[0, 4:5, :], mod_ref[0, 3:4, :])
    _store_row_slabs(hfp_ref, hf)

    logits = _dot_nt(wr_ref[...], hf.astype(BF16)) + br_ref[:, 0:1]
    n_exp = logits.shape[0]
    eio = lax.broadcasted_iota(I32, (n_exp, tm), 0)
    vals, idxs, hots = [], [], []
    rest = logits
    for _ in range(TOP_K):
        m = jnp.max(rest, axis=0, keepdims=True)
        idx = jnp.min(jnp.where(rest == m, eio, n_exp), axis=0, keepdims=True)
        hot = eio == idx
        rest = jnp.where(hot, -jnp.inf, rest)
        vals.append(m)
        idxs.append(idx)
        hots.append(hot)
    ex = [jnp.exp(vk - vals[0]) for vk in vals]
    den = ex[0] + ex[1] + ex[2] + ex[3]
    gates = [e / den for e in ex]

    picked = jnp.zeros((n_exp, tm), F32)
    for hot in hots:
        picked = picked + jnp.where(hot, 1.0, 0.0)
    earlier = lax.broadcasted_iota(I32, (tm, tm), 0) < lax.broadcasted_iota(I32, (tm, tm), 1)
    before = _dot(picked.astype(BF16), jnp.where(earlier, 1.0, 0.0).astype(BF16)) + cnt_ref[:, 0:1]
    ranks = [jnp.sum(jnp.where(hot, before, 0.0), axis=0, keepdims=True).astype(I32) for hot in hots]
    cnt_ref[...] = cnt_ref[...] + jnp.sum(picked, axis=1, keepdims=True)

    route_ref[...] = jnp.concatenate(idxs + ranks, axis=0)
    gate_ref[...] = jnp.concatenate(gates + [jnp.zeros_like(g) for g in gates], axis=0)


def _hgrn_out_kernel(of_ref, ob_ref, g_ref, x_ref, mod_ref, gout_ref, wout_ref, gffn_ref, wr_ref, br_ref,
                     x1_ref, hfp_ref, route_ref, gate_ref, cnt_ref):
    @pl.when(pl.program_id(0) == 0)
    def _():
        cnt_ref[...] = jnp.zeros_like(cnt_ref)

    o = of_ref[...] + ob_ref[...]
    parts = []
    for h in range(o.shape[1] // HEAD_DIM):
        oh = o[:, h * HEAD_DIM:(h + 1) * HEAD_DIM]
        parts.append(oh * lax.rsqrt(jnp.mean(oh * oh, axis=-1, keepdims=True) + EPS))
    on = jnp.concatenate(parts, axis=1) * gout_ref[...]
    g = g_ref[...].astype(F32)
    a = on * (g * jax.nn.sigmoid(g))
    y = _dot(a.astype(BF16), wout_ref[...])
    _ffn_route_tail(x_ref[...], y, mod_ref, gffn_ref, wr_ref, br_ref,
                    x1_ref, hfp_ref, route_ref, gate_ref, cnt_ref)


def _conv_out_kernel(a_ref, x_ref, mod_ref, wout_ref, bout_ref, gffn_ref, wr_ref, br_ref,
                     x1_ref, hfp_ref, route_ref, gate_ref, cnt_ref):
    @pl.when(pl.program_id(0) == 0)
    def _():
        cnt_ref[...] = jnp.zeros_like(cnt_ref)

    y = _dot(a_ref[...], wout_ref[...]) + bout_ref[...]
    _ffn_route_tail(x_ref[...], y, mod_ref, gffn_ref, wr_ref, br_ref,
                    x1_ref, hfp_ref, route_ref, gate_ref, cnt_ref)


def _tail_specs(n, d, n_exp, tm, tiles_per_batch):
    tok = lambda i: (i, 0)
    const2 = lambda i: (0, 0)
    specs = dict(
        tok=pl.BlockSpec((tm, d), tok),
        mod=pl.BlockSpec((1, 6, d), lambda i: (i // tiles_per_batch, 0, 0)),
        vec=pl.BlockSpec((1, d), const2),
        mat=pl.BlockSpec((d, d), const2),
        wr=pl.BlockSpec((n_exp, d), const2),
        br=pl.BlockSpec((n_exp, V7X_LANES), const2),
    )
    out_specs = [pl.BlockSpec((tm, d), tok),
                 pl.BlockSpec((tm, d // V7X_LANES, V7X_LANES), lambda i: (i, 0, 0)),
                 pl.BlockSpec((8, tm), lambda i: (0, i)),
                 pl.BlockSpec((8, tm), lambda i: (0, i)),
                 pl.BlockSpec((n_exp, V7X_LANES), const2)]
    out_shape = [jax.ShapeDtypeStruct((n, d), F32),
                 jax.ShapeDtypeStruct((n, d // V7X_LANES, V7X_LANES), BF16),
                 jax.ShapeDtypeStruct((8, n), I32),
                 jax.ShapeDtypeStruct((8, n), F32),
                 jax.ShapeDtypeStruct((n_exp, V7X_LANES), F32)]
    return specs, out_specs, out_shape


def _hgrn_out(o_f, o_b, g, x, mods, g_out, w_out, g_ffn, wr_t, br, tm):
    n, d = x.shape
    s, out_specs, out_shape = _tail_specs(n, d, wr_t.shape[0], tm, n // mods.shape[0] // tm)
    return pl.pallas_call(
        _hgrn_out_kernel,
        grid=(n // tm,),
        in_specs=[s["tok"], s["tok"], s["tok"], s["tok"], s["mod"], s["vec"], s["mat"], s["vec"], s["wr"], s["br"]],
        out_specs=out_specs,
        out_shape=out_shape,
        compiler_params=_params(("arbitrary",), 48 << 20),
        name="hgrn_out_route",
    )(o_f, o_b, g, x, mods, g_out, w_out, g_ffn, wr_t, br)


def _conv_out(a, x, mods, w_out, b_out, g_ffn, wr_t, br, tm):
    n, d = x.shape
    s, out_specs, out_shape = _tail_specs(n, d, wr_t.shape[0], tm, n // mods.shape[0] // tm)
    return pl.pallas_call(
        _conv_out_kernel,
        grid=(n // tm,),
        in_specs=[s["tok"], s["tok"], s["mod"], s["mat"], s["vec"], s["vec"], s["wr"], s["br"]],
        out_specs=out_specs,
        out_shape=out_shape,
        compiler_params=_params(("arbitrary",), 40 << 20),
        name="conv_out_route",
    )(a, x, mods, w_out, b_out, g_ffn, wr_t, br)


def _routing_plan(route, counts, n_blocks, tmb, tm):
    n_exp = counts.shape[0]
    padded = (counts + tmb - 1) // tmb * tmb
    pend = jnp.cumsum(padded)
    pstart = pend - padded
    n_valid = pend[-1] // tmb
    blk = jnp.arange(n_blocks, dtype=I32)
    blk_e = jnp.minimum(jnp.sum((blk[:, None] * tmb >= pend[None, :]).astype(I32), axis=1), n_exp - 1)
    blk_e = jnp.where(blk < n_valid, blk_e, blk_e[jnp.maximum(n_valid - 1, 0)])
    eidx, rank = route[:TOP_K], route[TOP_K:2 * TOP_K]
    hit = eidx[None] == jnp.arange(n_exp, dtype=I32)[:, None, None]
    dest = rank + jnp.sum(jnp.where(hit, pstart[:, None, None], 0), axis=0)
    n = dest.shape[1]
    dest_tiles = dest.reshape(TOP_K, n // tm, tm).transpose(1, 0, 2).reshape(-1)
    return dest_tiles.astype(I32), blk_e, n_valid.reshape(1).astype(I32), (pstart + counts).astype(I32)


def _dispatch_kernel(fill_ref, n_valid_ref, dest_ref, hfp_ref, xs_ref, zero_buf, sem, *, fill_rows):
    tm = hfp_ref.shape[0]

    @pl.when(pl.program_id(0) == 0)
    def _():
        zero_buf[...] = jnp.zeros_like(zero_buf)
        for e in range(fill_ref.shape[0]):
            pltpu.make_async_copy(zero_buf, xs_ref.at[pl.ds(fill_ref[e], fill_rows)], sem).start()
        for e in range(fill_ref.shape[0]):
            pltpu.make_async_copy(zero_buf, xs_ref.at[pl.ds(fill_ref[e], fill_rows)], sem).wait()

        def clear(b, carry):
            tail = pltpu.make_async_copy(zero_buf, xs_ref.at[pl.ds(b * fill_rows, fill_rows)], sem)
            tail.start()
            tail.wait()
            return carry

        lax.fori_loop(n_valid_ref[0], xs_ref.shape[0] // fill_rows, clear, 0)

    def issue(r, carry):
        for k in range(TOP_K):
            pltpu.make_async_copy(hfp_ref.at[r], xs_ref.at[dest_ref[k * tm + r]], sem).start()
        return carry

    lax.fori_loop(0, tm, issue, 0)
    for k in range(TOP_K):
        pltpu.make_async_copy(hfp_ref, xs_ref.at[pl.ds(0, tm)], sem).wait()


def _dispatch(fill, n_valid, dest_tiles, hfp, rows, fill_rows, tm):
    n, slabs, lanes = hfp.shape
    return pl.pallas_call(
        functools.partial(_dispatch_kernel, fill_rows=fill_rows),
        grid_spec=pltpu.PrefetchScalarGridSpec(
            num_scalar_prefetch=2,
            grid=(n // tm,),
            in_specs=[pl.BlockSpec((TOP_K * tm,), lambda i, fill, nv: (i,), memory_space=pltpu.SMEM),
                      pl.BlockSpec((tm, slabs, lanes), lambda i, fill, nv: (i, 0, 0))],
            out_specs=pl.BlockSpec(memory_space=pl.ANY),
            scratch_shapes=[pltpu.VMEM((fill_rows, slabs, lanes), hfp.dtype), pltpu.SemaphoreType.DMA(())]),
        out_shape=jax.ShapeDtypeStruct((rows, slabs, lanes), hfp.dtype),
        compiler_params=_params(("arbitrary",), 16 << 20),
        name="moe_dispatch",
    )(fill, n_valid, dest_tiles, hfp)


def _expert_kernel(blk_e_ref, n_valid_ref, xs_ref, w1_ref, b1_ref, w2_ref, b2_ref, ys_ref):
    del blk_e_ref
    live = pl.program_id(0) < n_valid_ref[0]

    @pl.when(live)
    def _():
        h = _dot(_load_row_slabs(xs_ref), w1_ref[0]) + b1_ref[0]
        de = h.shape[1] // 2
        gate = jnp.minimum(h[:, :de], SWIGLU_LIMIT)
        up = jnp.clip(h[:, de:], -SWIGLU_LIMIT, SWIGLU_LIMIT)
        act = (up + 1.0) * gate * jax.nn.sigmoid(SWIGLU_ALPHA * gate)
        _store_row_slabs(ys_ref, _dot(act.astype(BF16), w2_ref[0]) + b2_ref[0])

    @pl.when(jnp.logical_not(live))
    def _():
        ys_ref[...] = jnp.zeros_like(ys_ref)


def _experts(blk_e, n_valid, xs, w1, b1, w2, b2, n_blocks, tmb):
    n_exp, d, de2 = w1.shape
    slabs, lanes = xs.shape[1:]
    row_map = lambda i, be, nv: (jnp.minimum(i, nv[0] - 1), 0, 0)
    w_map = lambda i, be, nv: (be[i], 0, 0)
    return pl.pallas_call(
        _expert_kernel,
        grid_spec=pltpu.PrefetchScalarGridSpec(
            num_scalar_prefetch=2,
            grid=(n_blocks,),
            in_specs=[pl.BlockSpec((tmb, slabs, lanes), row_map),
                      pl.BlockSpec((1, d, de2), w_map),
                      pl.BlockSpec((1, 1, de2), w_map),
                      pl.BlockSpec((1, de2 // 2, d), w_map),
                      pl.BlockSpec((1, 1, d), w_map)],
            out_specs=pl.BlockSpec((tmb, slabs, lanes), lambda i, be, nv: (i, 0, 0))),
        out_shape=jax.ShapeDtypeStruct((n_blocks * tmb, slabs, lanes), xs.dtype),
        compiler_params=_params(("arbitrary",), 48 << 20),
        name="moe_experts",
    )(blk_e, n_valid, xs, w1, b1.reshape(n_exp, 1, de2), w2, b2.reshape(n_exp, 1, d))


def _gather_combine(dest_ref, ys_ref, gate_ref, buf, sem):
    tm = gate_ref.shape[0]

    def issue(r, carry):
        for k in range(TOP_K):
            pltpu.make_async_copy(ys_ref.at[dest_ref[k * tm + r]], buf.at[k, r], sem).start()
        return carry

    lax.fori_loop(0, tm, issue, 0)
    for k in range(TOP_K):
        pltpu.make_async_copy(ys_ref.at[pl.ds(0, tm)], buf.at[k], sem).wait()
    y = gate_ref[:, 0:1] * _load_row_slabs(buf.at[0]).astype(F32)
    for k in range(1, TOP_K):
        y = y + gate_ref[:, k:k + 1] * _load_row_slabs(buf.at[k]).astype(F32)
    return y


def _combine_conv_in_kernel(dest_ref, ys_ref, gate_ref, x_ref, mod_ref, modn_ref, gmix_ref, win_ref, bin_ref,
                            x2_ref, u_ref, buf, sem):
    y = _gather_combine(dest_ref, ys_ref, gate_ref, buf, sem)
    x2 = x_ref[...] + mod_ref[0, 5:6, :] * y
    x2_ref[...] = x2
    h = _rmsnorm_mod(x2, gmix_ref[...], modn_ref[0, 1:2, :], modn_ref[0, 0:1, :])
    p = _dot(h.astype(BF16), win_ref[...]) + bin_ref[...]
    d = x2.shape[1]
    u_ref[...] = p[:, :d] * jax.nn.sigmoid(p[:, d:])


def _combine_final_kernel(dest_ref, ys_ref, gate_ref, x_ref, mod_ref, gfin_ref, out_ref, buf, sem):
    y = _gather_combine(dest_ref, ys_ref, gate_ref, buf, sem)
    x4 = x_ref[...] + mod_ref[0, 5:6, :] * y
    out_ref[...] = x4 * lax.rsqrt(jnp.mean(x4 * x4, axis=-1, keepdims=True) + EPS) * gfin_ref[...]


def _combine_common(n, d, tm, tiles_per_batch):
    tok = lambda i: (i, 0)
    return dict(
        dest=pl.BlockSpec((TOP_K * tm,), lambda i: (i,), memory_space=pltpu.SMEM),
        ys=pl.BlockSpec(memory_space=pl.ANY),
        gate=pl.BlockSpec((tm, 8), tok),
        tok=pl.BlockSpec((tm, d), tok),
        mod=pl.BlockSpec((1, 6, d), lambda i: (i // tiles_per_batch, 0, 0)),
        vec=pl.BlockSpec((1, d), lambda i: (0, 0)),
        scratch=[pltpu.VMEM((TOP_K, tm, d // V7X_LANES, V7X_LANES), BF16), pltpu.SemaphoreType.DMA(())],
    )


def _combine_conv_in(dest_tiles, ys, gates, x, mods, mods_next, g_mix, w_in, b_in, tm):
    n, d = x.shape
    s = _combine_common(n, d, tm, n // mods.shape[0] // tm)
    return pl.pallas_call(
        _combine_conv_in_kernel,
        grid=(n // tm,),
        in_specs=[s["dest"], s["ys"], s["gate"], s["tok"], s["mod"], s["mod"], s["vec"],
                  pl.BlockSpec((d, 2 * d), lambda i: (0, 0)),
                  pl.BlockSpec((1, 2 * d), lambda i: (0, 0))],
        out_specs=[s["tok"], s["tok"]],
        out_shape=[jax.ShapeDtypeStruct((n, d), F32), jax.ShapeDtypeStruct((n, d), F32)],
        scratch_shapes=s["scratch"],
        compiler_params=_params(("arbitrary",), 40 << 20),
        name="moe_combine_conv_in",
    )(dest_tiles, ys, gates, x, mods, mods_next, g_mix, w_in, b_in)


def _combine_final(dest_tiles, ys, gates, x, mods, g_final, tm):
    n, d = x.shape
    s = _combine_common(n, d, tm, n // mods.shape[0] // tm)
    return pl.pallas_call(
        _combine_final_kernel,
        grid=(n // tm,),
        in_specs=[s["dest"], s["ys"], s["gate"], s["tok"], s["mod"], s["vec"]],
        out_specs=s["tok"],
        out_shape=jax.ShapeDtypeStruct((n, d), F32),
        scratch_shapes=s["scratch"],
        compiler_params=_params(("arbitrary",), 24 << 20),
        name="moe_combine_final",
    )(dest_tiles, ys, gates, x, mods, g_final)


def _moe_experts(route, counts, hfp, w1, b1, w2, b2, tmb, tm):
    n = hfp.shape[0]
    n_exp = w1.shape[0]
    n_blocks = (n * TOP_K + n_exp * (tmb - 1) + tmb - 1) // tmb
    dest_tiles, blk_e, n_valid, fill = _routing_plan(route, counts, n_blocks, tmb, tm)
    xs = _dispatch(fill, n_valid, dest_tiles, hfp, (n_blocks + 1) * tmb, tmb, tm)
    ys = _experts(blk_e, n_valid, xs, w1, b1, w2, b2, n_blocks, tmb)
    return dest_tiles, ys


def _dwconv_kernel(cur_ref, prev_ref, next_ref, w_ref, bdw_ref, gln_ref, bln_ref, out_ref, hpad, vbuf, zbuf):
    i = pl.program_id(1)
    rt, w, d = cur_ref.shape[1], cur_ref.shape[2], cur_ref.shape[3]
    half = d // 2
    pad = CONV_WIDTH // 2
    off = 16

    hpad[:, 0:off, :] = jnp.zeros((rt, off, half), F32)
    hpad[:, off + w:off + w + off, :] = jnp.zeros((rt, off, half), F32)
    hpad[:, off:off + w, :] = cur_ref[0, :, :, 0:half]
    vbuf[0:rt] = jnp.where(i > 0, prev_ref[0], 0.0)
    vbuf[rt:2 * rt] = cur_ref[0, :, :, half:d]
    vbuf[2 * rt:3 * rt] = jnp.where(i < pl.num_programs(1) - 1, next_ref[0], 0.0)

    def row_body(r, carry):
        for lg in range(half // V7X_LANES):
            ls = slice(lg * V7X_LANES, (lg + 1) * V7X_LANES)
            ls_hi = slice(half + lg * V7X_LANES, half + (lg + 1) * V7X_LANES)
            acc_h = jnp.zeros((w, V7X_LANES), F32)
            acc_v = jnp.zeros((w, V7X_LANES), F32)
            for j in range(CONV_WIDTH):
                acc_h = acc_h + hpad[r, off - pad + j:off - pad + j + w, ls] * w_ref[j:j + 1, ls]
                acc_v = acc_v + vbuf[rt - pad + j + r, :, ls] * w_ref[j:j + 1, ls_hi]
            zbuf[r, :, ls] = acc_h
            zbuf[r, :, ls_hi] = acc_v
        return carry

    lax.fori_loop(0, rt, row_body, 0)

    z = zbuf[...].reshape(rt * w, d) + bdw_ref[...]
    mu = jnp.mean(z, axis=-1, keepdims=True)
    zc = z - mu
    y = zc * lax.rsqrt(jnp.mean(zc * zc, axis=-1, keepdims=True) + EPS) * gln_ref[...] + bln_ref[...]
    out_ref[0] = (y * jax.nn.sigmoid(y)).astype(BF16).reshape(rt, w, d)


def _dwconv(u, w_dw, b_dw, g_ln, b_ln, rt):
    b, r, w, d = u.shape
    half = d // 2
    nb = r // rt
    vec = pl.BlockSpec((1, d), lambda bi, i: (0, 0))
    return pl.pallas_call(
        _dwconv_kernel,
        grid=(b, nb),
        in_specs=[pl.BlockSpec((1, rt, w, d), lambda bi, i: (bi, i, 0, 0)),
                  pl.BlockSpec((1, rt, w, half), lambda bi, i: (bi, jnp.maximum(i - 1, 0), 0, 1)),
                  pl.BlockSpec((1, rt, w, half), lambda bi, i: (bi, jnp.minimum(i + 1, nb - 1), 0, 1)),
                  pl.BlockSpec((CONV_WIDTH, d), lambda bi, i: (0, 0)),
                  vec, vec, vec],
        out_specs=pl.BlockSpec((1, rt, w, d), lambda bi, i: (bi, i, 0, 0)),
        out_shape=jax.ShapeDtypeStruct((b, r, w, d), BF16),
        scratch_shapes=[pltpu.VMEM((rt, w + 32, half), F32),
                        pltpu.VMEM((3 * rt, w, half), F32),
                        pltpu.VMEM((rt, w, d), F32)],
        compiler_params=_params(("arbitrary", "arbitrary"), 48 << 20),
        name="dwconv_ln_silu",
    )(u, u, u, w_dw, b_dw, g_ln, b_ln)


def kernel(x, c, ctx, c_ctx, w_ada, b_ada, g_mix, g_ffn, w_hgrn_in, hgrn_gamma, g_hgrn_out, w_hgrn_out,
           w_cv_in, b_cv_in, w_cv_dw, b_cv_dw, g_cv_ln, b_cv_ln, w_cv_out, b_cv_out,
           w_router, b_router, w_exp_in, b_exp_in, w_exp_out, b_exp_out, g_final):
    bsz, seq, d = x.shape
    n = bsz * seq
    n_heads = d // HEAD_DIM
    n_exp = w_router.shape[-1]
    assert bsz + 1 <= 8 and seq % 512 == 0 and ctx.shape[1] % 256 == 0

    cond = jnp.zeros((8, d), F32).at[:bsz].set(c).at[bsz].set(c_ctx)
    mods = _ada_mods(cond, w_ada, b_ada).reshape(w_ada.shape[0], 8, 6, d)
    mods_lat = [mods[l, :bsz] for l in range(w_ada.shape[0])]
    mods_ctx = jnp.broadcast_to(mods[0, bsz:bsz + 1], (bsz, 6, d))

    lb_all = jnp.cumsum(jax.nn.softmax(hgrn_gamma.astype(F32), axis=1), axis=1)
    lb = lb_all[:, 0, :]

    row = lambda v: v.reshape(1, -1)
    wr_t = [w_router[l].T.astype(BF16) for l in range(2)]
    br = [jnp.broadcast_to(b_router[l][:, None], (n_exp, V7X_LANES)) for l in range(2)]

    w_in = w_hgrn_in[0].astype(BF16)
    scan_t = 256
    zeros_state = jnp.zeros((bsz, n_heads, HEAD_DIM, HEAD_DIM), F32)
    pc = _hgrn_proj(ctx, mods_ctx, row(g_mix[0]), lb, w_in, 256)
    _, _, s_f, s_b = _gla_scan(pc[0], pc[1], pc[2], pc[3], pc[4], pc[5], zeros_state, zeros_state, scan_t, 2)
    pq, pkf, pkb, pv, pbf, pcb, pg = _hgrn_proj(x, mods_lat[0], row(g_mix[0]), lb, w_in, 256)
    o_f, o_b, _, _ = _gla_scan(pq, pkf, pkb, pv, pbf, pcb, s_f, s_b, scan_t, 2)

    tm_tail = 512
    x1, hfp, route, gates, cnt = _hgrn_out(
        o_f.reshape(n, d), o_b.reshape(n, d), pg.reshape(n, d), x.reshape(n, d), mods_lat[0],
        row(g_hgrn_out[0]), w_hgrn_out[0].astype(BF16), row(g_ffn[0]), wr_t[0], br[0], tm_tail)

    tmb, tm_rows = 512, 256
    dest_tiles, ys = _moe_experts(route, cnt[:, 0].astype(I32), hfp,
                                  w_exp_in[0].astype(BF16), b_exp_in[0], w_exp_out[0].astype(BF16), b_exp_out[0],
                                  tmb, tm_rows)

    x2, u = _combine_conv_in(dest_tiles, ys, gates.T, x1, mods_lat[0], mods_lat[1], row(g_mix[1]),
                             w_cv_in[0].astype(BF16), row(b_cv_in[0]), tm_rows)
    za = _dwconv(u.reshape(bsz, seq // GRID_W, GRID_W, d), w_cv_dw[0], row(b_cv_dw[0]),
                 row(g_cv_ln[0]), row(b_cv_ln[0]), 16)
    x3, hfp, route, gates, cnt = _conv_out(
        za.reshape(n, d), x2, mods_lat[1], w_cv_out[0].astype(BF16), row(b_cv_out[0]),
        row(g_ffn[1]), wr_t[1], br[1], tm_tail)
    dest_tiles, ys = _moe_experts(route, cnt[:, 0].astype(I32), hfp,
                                  w_exp_in[1].astype(BF16), b_exp_in[1], w_exp_out[1].astype(BF16), b_exp_out[1],
                                  tmb, tm_rows)
    out = _combine_final(dest_tiles, ys, gates.T, x3, mods_lat[1], row(g_final), tm_rows)
    return out.reshape(bsz, seq, d)
```

```python
import functools

import jax
import jax.numpy as jnp
from jax import lax
from jax.experimental import pallas as pl
from jax.experimental.pallas import tpu as pltpu

F32 = jnp.float32
BF16 = jnp.bfloat16
I32 = jnp.int32
U32 = jnp.uint32

EPS = 1e-6
HEAD_DIM = 128
GLA_CHUNK = 64
GLA_DECAY_LIMIT = 60.0
DMA_ISSUE_UNROLL = 8
GRID_W = 64
CONV_WIDTH = 31
TOP_K = 4
SWIGLU_ALPHA = 1.702
SWIGLU_LIMIT = 7.0

V7X_LANES = 128
V7X_VMEM_BYTES = 64 * 1024 * 1024
VMEM_HEADROOM_BYTES = 8 * 1024 * 1024


def _params(semantics, vmem_bytes):
    limit = min(int(vmem_bytes), V7X_VMEM_BYTES - VMEM_HEADROOM_BYTES)
    return pltpu.CompilerParams(dimension_semantics=semantics, vmem_limit_bytes=limit)


def _dot(a, b):
    return jnp.dot(a, b, preferred_element_type=F32)


def _dot_nt(a, b):
    return lax.dot_general(a, b, (((1,), (1,)), ((), ())), preferred_element_type=F32)


def _dot_tn(a, b):
    return lax.dot_general(a, b, (((0,), (0,)), ((), ())), preferred_element_type=F32)


def _rmsnorm_mod(x, g, scale, shift):
    y = x * lax.rsqrt(jnp.mean(x * x, axis=-1, keepdims=True) + EPS)
    return y * g * (1.0 + scale) + shift


def _store_row_slabs(ref, x):
    m, d = x.shape
    slabs = d // V7X_LANES
    for s in range(slabs):
        ref[pl.ds(s, m, stride=slabs), :] = x[:, s * V7X_LANES:(s + 1) * V7X_LANES]


def _load_row_slabs(ref, slabs, *lead):
    m = ref.shape[-2] // slabs
    return jnp.concatenate([ref[(*lead, pl.ds(s, m, stride=slabs), slice(None))] for s in range(slabs)], axis=1)


def _slab(row, slabs):
    return pl.ds(pl.multiple_of(row * slabs, slabs), slabs)


def _ada_kernel(c_ref, w_ref, b_ref, o_ref):
    s = c_ref[...]
    s = s * jax.nn.sigmoid(s)
    o_ref[0] = jnp.dot(s, w_ref[0], preferred_element_type=F32,
                       precision=lax.Precision.HIGHEST) + b_ref[0]


def _ada_mods(cond, w_ada, b_ada):
    depth, d, d6 = w_ada.shape
    rows = cond.shape[0]
    tn = 1536
    return pl.pallas_call(
        _ada_kernel,
        grid=(depth, d6 // tn),
        in_specs=[pl.BlockSpec((rows, d), lambda l, n: (0, 0)),
                  pl.BlockSpec((1, d, tn), lambda l, n: (l, 0, n)),
                  pl.BlockSpec((1, 1, tn), lambda l, n: (l, 0, n))],
        out_specs=pl.BlockSpec((1, rows, tn), lambda l, n: (l, 0, n)),
        out_shape=jax.ShapeDtypeStruct((depth, rows, d6), F32),
        compiler_params=_params(("arbitrary", "arbitrary"), 24 << 20),
        name="ada_mods",
    )(cond, w_ada, b_ada.reshape(depth, 1, d6))


def _chunk_cumsum(x, chunk, reverse):
    rows, cols = x.shape
    pos = lax.broadcasted_iota(I32, (rows, V7X_LANES), 0) & (chunk - 1)
    step = 1
    while step < chunk:
        keep = (pos < chunk - step) if reverse else (pos >= step)
        shift = rows - step if reverse else step
        parts = []
        for c in range(cols // V7X_LANES):
            xs = x[:, c * V7X_LANES:(c + 1) * V7X_LANES]
            parts.append(xs + jnp.where(keep, pltpu.roll(xs, shift, 0), 0.0))
        x = jnp.concatenate(parts, axis=1)
        step *= 2
    return x


def _hgrn_proj_kernel(x_ref, mod_ref, gmix_ref, lb_ref, w_ref,
                      q_ref, kf_ref, kb_ref, v_ref, bf_ref, cb_ref, g_ref):
    d = x_ref.shape[-1]
    h = _rmsnorm_mod(x_ref[0], gmix_ref[...], mod_ref[0, 1:2, :], mod_ref[0, 0:1, :])
    p = _dot(h.astype(BF16), w_ref[...])
    q_ref[0] = p[:, 0:d].astype(BF16)
    v_ref[0] = p[:, d:2 * d].astype(BF16)
    g_ref[0] = p[:, 4 * d:5 * d].astype(BF16)
    lbf = lb_ref[0:1, :]
    lbb = lb_ref[1:2, :]
    ff = lbf + (1.0 - lbf) * jax.nn.sigmoid(p[:, 2 * d:3 * d])
    fb = lbb + (1.0 - lbb) * jax.nn.sigmoid(p[:, 3 * d:4 * d])
    kf_ref[0] = (1.0 - ff).astype(BF16)
    kb_ref[0] = (1.0 - fb).astype(BF16)
    bf_ref[0] = _chunk_cumsum(jnp.log(ff), GLA_CHUNK, reverse=False)
    cb_ref[0] = _chunk_cumsum(jnp.log(fb), GLA_CHUNK, reverse=True)


def _hgrn_proj(x, mods, g_mix, lb, w_in, tm):
    b, l, d = x.shape
    d5 = w_in.shape[1]
    tok = lambda bi, i: (bi, i, 0)
    const2 = lambda bi, i: (0, 0)
    bf_sd = jax.ShapeDtypeStruct((b, l, d), BF16)
    f_sd = jax.ShapeDtypeStruct((b, l, d), F32)
    return pl.pallas_call(
        _hgrn_proj_kernel,
        grid=(b, l // tm),
        in_specs=[pl.BlockSpec((1, tm, d), tok),
                  pl.BlockSpec((1, 6, d), lambda bi, i: (bi, 0, 0)),
                  pl.BlockSpec((1, d), const2),
                  pl.BlockSpec((2, d), const2),
                  pl.BlockSpec((d, d5), const2)],
        out_specs=[pl.BlockSpec((1, tm, d), tok)] * 7,
        out_shape=[bf_sd, bf_sd, bf_sd, bf_sd, f_sd, f_sd, bf_sd],
        compiler_params=_params(("arbitrary", "arbitrary"), 52 << 20),
        name="hgrn_proj",
    )(x, mods, g_mix, lb, w_in)


def _gla_chunk(q, k, v, cum, st, *, reverse, stable):
    c = q.shape[0]
    qf = q.astype(F32)
    kf = k.astype(F32)
    last = cum[0:1] if reverse else cum[c - 1:c]
    row = lax.broadcasted_iota(I32, (c, c), 0)
    col = lax.broadcasted_iota(I32, (c, c), 1)
    if not stable:
        mid_row = c // 2 if reverse else c // 2 - 1
        mid = cum[mid_row:mid_row + 1]
        x = cum - mid
        qe = qf * jnp.exp(x)
        ke = kf * jnp.exp(-x)
        a = _dot_nt(qe.astype(BF16), ke.astype(BF16))
        a = jnp.where((col >= row) if reverse else (row >= col), a, 0.0)
        qh = (qe * jnp.exp(mid)).astype(BF16)
        kh = (ke * jnp.exp(last - mid)).astype(BF16)
    else:
        a = jnp.where(row == col, _dot_nt(q, k), 0.0)
        c1 = cum.astype(BF16)
        r1 = cum - c1.astype(F32)
        c2 = r1.astype(BF16)
        c3 = (r1 - c2.astype(F32)).astype(BF16)
        level = 0
        while (1 << level) < c:
            m = 1 << level
            seg_row = row >> (level + 1)
            ref_row = (seg_row << (level + 1)) + (m if reverse else m - 1)
            sel = jnp.where(col == ref_row, 1.0, 0.0).astype(BF16)
            cref = _dot(sel, c1) + _dot(sel, c2) + _dot(sel, c3)
            e = jnp.exp(-jnp.abs(cum - cref))
            contrib = _dot_nt((qf * e).astype(BF16), (kf * e).astype(BF16))
            row_bit = (row >> level) & 1
            col_bit = (col >> level) & 1
            same = seg_row == (col >> (level + 1))
            if reverse:
                pick = jnp.where(same, (1 - row_bit) * col_bit, 0)
            else:
                pick = jnp.where(same, row_bit * (1 - col_bit), 0)
            a = a + jnp.where(pick == 1, contrib, 0.0)
            level += 1
        qh = (qf * jnp.exp(cum)).astype(BF16)
        kh = (kf * jnp.exp(last - cum)).astype(BF16)
    o = _dot(a.astype(BF16), v) + _dot_nt(qh, st.astype(BF16))
    st_new = st * jnp.exp(last) + _dot_tn(v, kh)
    return o, st_new


def _gla_scan_kernel(qf_ref, kf_ref, vf_ref, bf_ref, qb_ref, kb_ref, vb_ref, cb_ref, s0f_ref, s0b_ref,
                     of_ref, ob_ref, sf_ref, sb_ref):
    j = pl.program_id(2)

    @pl.when(j == 0)
    def _():
        sf_ref[...] = s0f_ref[...]
        sb_ref[...] = s0b_ref[...]

    t = qf_ref.shape[1]
    n_chunks = t // GLA_CHUNK
    n_heads = qf_ref.shape[2] // HEAD_DIM
    half = GLA_CHUNK // 2

    worst = jnp.zeros((1, qf_ref.shape[2]), F32)
    for c in range(n_chunks):
        lo = c * GLA_CHUNK
        bm = bf_ref[0, lo + half - 1:lo + half, :]
        bl = bf_ref[0, lo + GLA_CHUNK - 1:lo + GLA_CHUNK, :]
        cm = cb_ref[0, lo + half:lo + half + 1, :]
        c0 = cb_ref[0, lo:lo + 1, :]
        worst = jnp.maximum(jnp.maximum(worst, jnp.maximum(-bm, bm - bl)), jnp.maximum(-cm, cm - c0))
    steep = jnp.max(worst) > GLA_DECAY_LIMIT

    def run(stable):
        for h in range(n_heads):
            ls = slice(h * HEAD_DIM, (h + 1) * HEAD_DIM)
            st = sf_ref[0, h]
            for c in range(n_chunks):
                rs = slice(c * GLA_CHUNK, (c + 1) * GLA_CHUNK)
                o, st = _gla_chunk(qf_ref[0, rs, ls], kf_ref[0, rs, ls], vf_ref[0, rs, ls], bf_ref[0, rs, ls],
                                   st, reverse=False, stable=stable)
                of_ref[0, rs, ls] = o
            sf_ref[0, h] = st
            st = sb_ref[0, h]
            for c in reversed(range(n_chunks)):
                rs = slice(c * GLA_CHUNK, (c + 1) * GLA_CHUNK)
                o, st = _gla_chunk(qb_ref[0, rs, ls], kb_ref[0, rs, ls], vb_ref[0, rs, ls], cb_ref[0, rs, ls],
                                   st, reverse=True, stable=stable)
                ob_ref[0, rs, ls] = o
            sb_ref[0, h] = st

    @pl.when(steep)
    def _():
        run(True)

    @pl.when(jnp.logical_not(steep))
    def _():
        run(False)


def _gla_scan(q, kf, kb, v, bf, cb, s0f, s0b, t, heads_per_step):
    b, l, d = q.shape
    lanes = heads_per_step * HEAD_DIM
    nblk = l // t
    fwd = lambda bi, hp, j: (bi, j, hp)
    bwd = lambda bi, hp, j: (bi, nblk - 1 - j, hp)
    st_map = lambda bi, hp, j: (bi, hp, 0, 0)
    tok_f = pl.BlockSpec((1, t, lanes), fwd)
    tok_b = pl.BlockSpec((1, t, lanes), bwd)
    st_spec = pl.BlockSpec((1, heads_per_step, HEAD_DIM, HEAD_DIM), st_map)
    o_sd = jax.ShapeDtypeStruct((b, l, d), F32)
    s_sd = jax.ShapeDtypeStruct(s0f.shape, F32)
    return pl.pallas_call(
        _gla_scan_kernel,
        grid=(b, d // lanes, nblk),
        in_specs=[tok_f, tok_f, tok_f, tok_f, tok_b, tok_b, tok_b, tok_b, st_spec, st_spec],
        out_specs=[tok_f, tok_b, st_spec, st_spec],
        out_shape=[o_sd, o_sd, s_sd, s_sd],
        compiler_params=_params(("arbitrary", "arbitrary", "arbitrary"), 32 << 20),
        name="gla_scan",
    )(q, kf, v, bf, q, kb, v, cb, s0f, s0b)


def _ffn_route_tail(x, y, mod_ref, gffn_ref, wr_ref, br_ref, x1_ref, hfp_ref, route_ref, gate_ref, cnt_ref):
    tm = x.shape[0]
    x1 = x + mod_ref[0, 2:3, :] * y
    x1_ref[...] = x1
    hf = _rmsnorm_mod(x1, gffn_ref[...], mod_ref[0, 4:5, :], mod_ref[0, 3:4, :])
    _store_row_slabs(hfp_ref, hf)

    logits = _dot_nt(wr_ref[...], hf.astype(BF16)) + br_ref[:, 0:1]
    n_exp = logits.shape[0]
    eio = lax.broadcasted_iota(I32, (n_exp, tm), 0)
    vals, idxs, hots = [], [], []
    rest = logits
    for _ in range(TOP_K):
        m = jnp.max(rest, axis=0, keepdims=True)
        idx = jnp.min(jnp.where(rest == m, eio, n_exp), axis=0, keepdims=True)
        hot = eio == idx
        rest = jnp.where(hot, -jnp.inf, rest)
        vals.append(m)
        idxs.append(idx)
        hots.append(hot)
    ex = [jnp.exp(vk - vals[0]) for vk in vals]
    den = ex[0] + ex[1] + ex[2] + ex[3]
    gates = [e / den for e in ex]

    picked = jnp.zeros((n_exp, tm), F32)
    for hot in hots:
        picked = picked + jnp.where(hot, 1.0, 0.0)
    earlier = lax.broadcasted_iota(I32, (tm, tm), 0) < lax.broadcasted_iota(I32, (tm, tm), 1)
    before = _dot(picked.astype(BF16), jnp.where(earlier, 1.0, 0.0).astype(BF16)) + cnt_ref[:, 0:1]
    ranks = [jnp.sum(jnp.where(hot, before, 0.0), axis=0, keepdims=True).astype(I32) for hot in hots]
    cnt_ref[...] = cnt_ref[...] + jnp.sum(picked, axis=1, keepdims=True)

    route_ref[...] = jnp.concatenate(idxs + ranks, axis=0)
    gate_ref[...] = jnp.concatenate(gates + [jnp.zeros_like(g) for g in gates], axis=0)


def _hgrn_out_kernel(of_ref, ob_ref, g_ref, x_ref, mod_ref, gout_ref, wout_ref, gffn_ref, wr_ref, br_ref,
                     x1_ref, hfp_ref, route_ref, gate_ref, cnt_ref):
    @pl.when(pl.program_id(0) == 0)
    def _():
        cnt_ref[...] = jnp.zeros_like(cnt_ref)

    o = of_ref[...] + ob_ref[...]
    parts = []
    for h in range(o.shape[1] // HEAD_DIM):
        oh = o[:, h * HEAD_DIM:(h + 1) * HEAD_DIM]
        parts.append(oh * lax.rsqrt(jnp.mean(oh * oh, axis=-1, keepdims=True) + EPS))
    on = jnp.concatenate(parts, axis=1) * gout_ref[...]
    g = g_ref[...].astype(F32)
    a = on * (g * jax.nn.sigmoid(g))
    y = _dot(a.astype(BF16), wout_ref[...])
    _ffn_route_tail(x_ref[...], y, mod_ref, gffn_ref, wr_ref, br_ref,
                    x1_ref, hfp_ref, route_ref, gate_ref, cnt_ref)


def _conv_out_kernel(a_ref, x_ref, mod_ref, wout_ref, bout_ref, gffn_ref, wr_ref, br_ref,
                     x1_ref, hfp_ref, route_ref, gate_ref, cnt_ref):
    @pl.when(pl.program_id(0) == 0)
    def _():
        cnt_ref[...] = jnp.zeros_like(cnt_ref)

    y = _dot(a_ref[...], wout_ref[...]) + bout_ref[...]
    _ffn_route_tail(x_ref[...], y, mod_ref, gffn_ref, wr_ref, br_ref,
                    x1_ref, hfp_ref, route_ref, gate_ref, cnt_ref)


def _tail_specs(n, d, n_exp, tm, tiles_per_batch):
    tok = lambda i: (i, 0)
    const2 = lambda i: (0, 0)
    specs = dict(
        tok=pl.BlockSpec((tm, d), tok),
        mod=pl.BlockSpec((1, 6, d), lambda i: (i // tiles_per_batch, 0, 0)),
        vec=pl.BlockSpec((1, d), const2),
        mat=pl.BlockSpec((d, d), const2),
        wr=pl.BlockSpec((n_exp, d), const2),
        br=pl.BlockSpec((n_exp, V7X_LANES), const2),
    )
    out_specs = [pl.BlockSpec((tm, d), tok),
                 pl.BlockSpec((tm * (d // V7X_LANES), V7X_LANES), tok),
                 pl.BlockSpec((8, tm), lambda i: (0, i)),
                 pl.BlockSpec((8, tm), lambda i: (0, i)),
                 pl.BlockSpec((n_exp, V7X_LANES), const2)]
    out_shape = [jax.ShapeDtypeStruct((n, d), F32),
                 jax.ShapeDtypeStruct((n * (d // V7X_LANES), V7X_LANES), F32),
                 jax.ShapeDtypeStruct((8, n), I32),
                 jax.ShapeDtypeStruct((8, n), F32),
                 jax.ShapeDtypeStruct((n_exp, V7X_LANES), F32)]
    return specs, out_specs, out_shape


def _hgrn_out(o_f, o_b, g, x, mods, g_out, w_out, g_ffn, wr_t, br, tm):
    n, d = x.shape
    s, out_specs, out_shape = _tail_specs(n, d, wr_t.shape[0], tm, n // mods.shape[0] // tm)
    return pl.pallas_call(
        _hgrn_out_kernel,
        grid=(n // tm,),
        in_specs=[s["tok"], s["tok"], s["tok"], s["tok"], s["mod"], s["vec"], s["mat"], s["vec"], s["wr"], s["br"]],
        out_specs=out_specs,
        out_shape=out_shape,
        compiler_params=_params(("arbitrary",), 48 << 20),
        name="hgrn_out_route",
    )(o_f, o_b, g, x, mods, g_out, w_out, g_ffn, wr_t, br)


def _conv_out(a, x, mods, w_out, b_out, g_ffn, wr_t, br, tm):
    n, d = x.shape
    s, out_specs, out_shape = _tail_specs(n, d, wr_t.shape[0], tm, n // mods.shape[0] // tm)
    return pl.pallas_call(
        _conv_out_kernel,
        grid=(n // tm,),
        in_specs=[s["tok"], s["tok"], s["mod"], s["mat"], s["vec"], s["vec"], s["wr"], s["br"]],
        out_specs=out_specs,
        out_shape=out_shape,
        compiler_params=_params(("arbitrary",), 40 << 20),
        name="conv_out_route",
    )(a, x, mods, w_out, b_out, g_ffn, wr_t, br)


def _routing_plan(route, counts, n_blocks, tmb, tm):
    n_exp = counts.shape[0]
    padded = (counts + tmb - 1) // tmb * tmb
    pend = jnp.cumsum(padded)
    pstart = pend - padded
    n_valid = pend[-1] // tmb
    blk = jnp.arange(n_blocks, dtype=I32)
    blk_e = jnp.minimum(jnp.sum((blk[:, None] * tmb >= pend[None, :]).astype(I32), axis=1), n_exp - 1)
    blk_e = jnp.where(blk < n_valid, blk_e, blk_e[jnp.maximum(n_valid - 1, 0)])
    eidx, rank = route[:TOP_K], route[TOP_K:2 * TOP_K]
    hit = eidx[None] == jnp.arange(n_exp, dtype=I32)[:, None, None]
    dest = rank + jnp.sum(jnp.where(hit, pstart[:, None, None], 0), axis=0)
    n = dest.shape[1]
    dest_tiles = dest.reshape(TOP_K, n // tm, tm).transpose(1, 0, 2).reshape(-1)
    return dest_tiles.astype(I32), blk_e, n_valid.reshape(1).astype(I32), (pstart + counts).astype(I32)


def _dispatch_kernel(fill_ref, n_valid_ref, dest_ref, hfp_ref, xs_ref, zero_buf, sem, *, slabs):
    tm = hfp_ref.shape[0] // slabs
    fill_rows = zero_buf.shape[0] // slabs

    @pl.when(pl.program_id(0) == 0)
    def _():
        zero_buf[...] = jnp.zeros_like(zero_buf)

        def pad_rows(start):
            return xs_ref.at[pl.ds(pl.multiple_of(start * slabs, slabs), fill_rows * slabs)]

        for e in range(fill_ref.shape[0]):
            pltpu.make_async_copy(zero_buf, pad_rows(fill_ref[e]), sem).start()
        for e in range(fill_ref.shape[0]):
            pltpu.make_async_copy(zero_buf, pad_rows(fill_ref[e]), sem).wait()

        def clear(b, carry):
            tail = pltpu.make_async_copy(zero_buf, pad_rows(b * fill_rows), sem)
            tail.start()
            tail.wait()
            return carry

        lax.fori_loop(n_valid_ref[0], xs_ref.shape[0] // (fill_rows * slabs), clear, 0)

    def issue(r, carry):
        for k in range(TOP_K):
            pltpu.make_async_copy(hfp_ref.at[_slab(r, slabs)], xs_ref.at[_slab(dest_ref[k * tm + r], slabs)], sem).start()
        return carry

    lax.fori_loop(0, tm, issue, 0, unroll=DMA_ISSUE_UNROLL)
    for k in range(TOP_K):
        pltpu.make_async_copy(hfp_ref, xs_ref.at[pl.ds(0, tm * slabs)], sem).wait()


def _dispatch(fill, n_valid, dest_tiles, hfp, slabs, rows, fill_rows, tm):
    lanes = hfp.shape[1]
    return pl.pallas_call(
        functools.partial(_dispatch_kernel, slabs=slabs),
        grid_spec=pltpu.PrefetchScalarGridSpec(
            num_scalar_prefetch=2,
            grid=(hfp.shape[0] // (tm * slabs),),
            in_specs=[pl.BlockSpec((TOP_K * tm,), lambda i, fill, nv: (i,), memory_space=pltpu.SMEM),
                      pl.BlockSpec((tm * slabs, lanes), lambda i, fill, nv: (i, 0))],
            out_specs=pl.BlockSpec(memory_space=pl.ANY),
            scratch_shapes=[pltpu.VMEM((fill_rows * slabs, lanes), hfp.dtype), pltpu.SemaphoreType.DMA(())]),
        out_shape=jax.ShapeDtypeStruct((rows * slabs, lanes), hfp.dtype),
        compiler_params=_params(("arbitrary",), 16 << 20),
        name="moe_dispatch",
    )(fill, n_valid, dest_tiles, hfp)


def _expert_kernel(blk_e_ref, n_valid_ref, xs_ref, w1_ref, b1_ref, w2_ref, b2_ref, ys_ref, w1_bf, w2_bf):
    i = pl.program_id(0)
    live = i < n_valid_ref[0]
    new_expert = jnp.logical_or(i == 0, blk_e_ref[i] != blk_e_ref[jnp.maximum(i - 1, 0)])

    @pl.when(jnp.logical_and(live, new_expert))
    def _():
        w1_bf[...] = w1_ref[0, 0].astype(BF16)
        w2_bf[...] = w2_ref[0, 0].astype(BF16)

    @pl.when(live)
    def _():
        slabs = w1_bf.shape[0] // V7X_LANES
        h = _dot(_load_row_slabs(xs_ref, slabs).astype(BF16), w1_bf[...]) + b1_ref[0, 0]
        de = h.shape[1] // 2
        gate = jnp.minimum(h[:, :de], SWIGLU_LIMIT)
        up = jnp.clip(h[:, de:], -SWIGLU_LIMIT, SWIGLU_LIMIT)
        act = (up + 1.0) * gate * jax.nn.sigmoid(SWIGLU_ALPHA * gate)
        _store_row_slabs(ys_ref, _dot(act.astype(BF16), w2_bf[...]) + b2_ref[0, 0])

    @pl.when(jnp.logical_not(live))
    def _():
        ys_ref[...] = jnp.zeros_like(ys_ref)


def _experts(blk_e, n_valid, xs, layer, w1, b1, w2, b2, n_blocks, tmb):
    _, n_exp, d, de2 = w1.shape
    slabs = d // V7X_LANES
    row_map = lambda i, be, nv: (jnp.minimum(i, nv[0] - 1), 0)
    w_map = lambda i, be, nv: (layer, be[i], 0, 0)
    return pl.pallas_call(
        _expert_kernel,
        grid_spec=pltpu.PrefetchScalarGridSpec(
            num_scalar_prefetch=2,
            grid=(n_blocks,),
            in_specs=[pl.BlockSpec((tmb * slabs, V7X_LANES), row_map),
                      pl.BlockSpec((1, 1, d, de2), w_map),
                      pl.BlockSpec((1, 1, 1, de2), w_map),
                      pl.BlockSpec((1, 1, de2 // 2, d), w_map),
                      pl.BlockSpec((1, 1, 1, d), w_map)],
            out_specs=pl.BlockSpec((tmb * slabs, V7X_LANES), lambda i, be, nv: (i, 0)),
            scratch_shapes=[pltpu.VMEM((d, de2), BF16), pltpu.VMEM((de2 // 2, d), BF16)]),
        out_shape=jax.ShapeDtypeStruct((n_blocks * tmb * slabs, V7X_LANES), xs.dtype),
        compiler_params=_params(("arbitrary",), 56 << 20),
        name="moe_experts",
    )(blk_e, n_valid, xs, w1, b1.reshape(-1, n_exp, 1, de2), w2, b2.reshape(-1, n_exp, 1, d))


def _gather_start(dest_ref, ys_ref, buf, sem, slot, tm, slabs):
    def issue(r, carry):
        for k in range(TOP_K):
            pltpu.make_async_copy(ys_ref.at[_slab(dest_ref[k * tm + r], slabs)],
                                  buf.at[slot, k, _slab(r, slabs)], sem.at[slot]).start()
        return carry

    lax.fori_loop(0, tm, issue, 0, unroll=DMA_ISSUE_UNROLL)


def _gather_combine(dest_ref, dest_next_ref, ys_ref, gate_ref, buf, sem):
    i = pl.program_id(0)
    slot = lax.rem(i, 2)
    tm = gate_ref.shape[0]
    slabs = buf.shape[2] // tm

    @pl.when(i == 0)
    def _():
        _gather_start(dest_ref, ys_ref, buf, sem, 0, tm, slabs)

    @pl.when(i + 1 < pl.num_programs(0))
    def _():
        _gather_start(dest_next_ref, ys_ref, buf, sem, 1 - slot, tm, slabs)

    for k in range(TOP_K):
        pltpu.make_async_copy(ys_ref.at[pl.ds(0, tm * slabs)], buf.at[slot, k], sem.at[slot]).wait()
    y = None
    for k in range(TOP_K):
        rows = _load_row_slabs(buf, slabs, slot, k)
        y = gate_ref[:, k:k + 1] * rows if y is None else y + gate_ref[:, k:k + 1] * rows
    return y


def _combine_conv_in_kernel(dest_ref, dest_next_ref, ys_ref, gate_ref, x_ref, mod_ref, modn_ref, gmix_ref,
                            win_ref, bin_ref, x2_ref, u_ref, buf, sem):
    y = _gather_combine(dest_ref, dest_next_ref, ys_ref, gate_ref, buf, sem)
    x2 = x_ref[...] + mod_ref[0, 5:6, :] * y
    x2_ref[...] = x2
    h = _rmsnorm_mod(x2, gmix_ref[...], modn_ref[0, 1:2, :], modn_ref[0, 0:1, :])
    p = _dot(h.astype(BF16), win_ref[...]) + bin_ref[...]
    d = x2.shape[1]
    u_ref[...] = p[:, :d] * jax.nn.sigmoid(p[:, d:])


def _combine_final_kernel(dest_ref, dest_next_ref, ys_ref, gate_ref, x_ref, mod_ref, gfin_ref, out_ref, buf, sem):
    y = _gather_combine(dest_ref, dest_next_ref, ys_ref, gate_ref, buf, sem)
    x4 = x_ref[...] + mod_ref[0, 5:6, :] * y
    out_ref[...] = x4 * lax.rsqrt(jnp.mean(x4 * x4, axis=-1, keepdims=True) + EPS) * gfin_ref[...]


def _combine_common(n, d, tm, tiles_per_batch):
    tok = lambda i: (i, 0)
    last = n // tm - 1
    return dict(
        dest=pl.BlockSpec((TOP_K * tm,), lambda i: (i,), memory_space=pltpu.SMEM),
        dest_next=pl.BlockSpec((TOP_K * tm,), lambda i: (jnp.minimum(i + 1, last),), memory_space=pltpu.SMEM),
        ys=pl.BlockSpec(memory_space=pl.ANY),
        gate=pl.BlockSpec((tm, 8), tok),
        tok=pl.BlockSpec((tm, d), tok),
        mod=pl.BlockSpec((1, 6, d), lambda i: (i // tiles_per_batch, 0, 0)),
        vec=pl.BlockSpec((1, d), lambda i: (0, 0)),
        scratch=[pltpu.VMEM((2, TOP_K, tm * (d // V7X_LANES), V7X_LANES), F32), pltpu.SemaphoreType.DMA((2,))],
    )


def _combine_conv_in(dest_tiles, ys, gates, x, mods, mods_next, g_mix, w_in, b_in, tm):
    n, d = x.shape
    s = _combine_common(n, d, tm, n // mods.shape[0] // tm)
    return pl.pallas_call(
        _combine_conv_in_kernel,
        grid=(n // tm,),
        in_specs=[s["dest"], s["dest_next"], s["ys"], s["gate"], s["tok"], s["mod"], s["mod"], s["vec"],
                  pl.BlockSpec((d, 2 * d), lambda i: (0, 0)),
                  pl.BlockSpec((1, 2 * d), lambda i: (0, 0))],
        out_specs=[s["tok"], s["tok"]],
        out_shape=[jax.ShapeDtypeStruct((n, d), F32), jax.ShapeDtypeStruct((n, d), F32)],
        scratch_shapes=s["scratch"],
        compiler_params=_params(("arbitrary",), 48 << 20),
        name="moe_combine_conv_in",
    )(dest_tiles, dest_tiles, ys, gates, x, mods, mods_next, g_mix, w_in, b_in)


def _combine_final(dest_tiles, ys, gates, x, mods, g_final, tm):
    n, d = x.shape
    s = _combine_common(n, d, tm, n // mods.shape[0] // tm)
    return pl.pallas_call(
        _combine_final_kernel,
        grid=(n // tm,),
        in_specs=[s["dest"], s["dest_next"], s["ys"], s["gate"], s["tok"], s["mod"], s["vec"]],
        out_specs=s["tok"],
        out_shape=jax.ShapeDtypeStruct((n, d), F32),
        scratch_shapes=s["scratch"],
        compiler_params=_params(("arbitrary",), 32 << 20),
        name="moe_combine_final",
    )(dest_tiles, dest_tiles, ys, gates, x, mods, g_final)


def _moe_experts(route, counts, hfp, layer, w1, b1, w2, b2, tmb, tm):
    slabs = w1.shape[2] // V7X_LANES
    n = hfp.shape[0] // slabs
    n_exp = w1.shape[1]
    n_blocks = (n * TOP_K + n_exp * (tmb - 1) + tmb - 1) // tmb
    dest_tiles, blk_e, n_valid, fill = _routing_plan(route, counts, n_blocks, tmb, tm)
    xs = _dispatch(fill, n_valid, dest_tiles, hfp, slabs, (n_blocks + 1) * tmb, tmb, tm)
    ys = _experts(blk_e, n_valid, xs, layer, w1, b1, w2, b2, n_blocks, tmb)
    return dest_tiles, ys


def _dwconv_kernel(cur_ref, prev_ref, next_ref, w_ref, bdw_ref, gln_ref, bln_ref, out_ref, hpad, vbuf, zbuf):
    i = pl.program_id(1)
    rt, w, d = cur_ref.shape[1], cur_ref.shape[2], cur_ref.shape[3]
    half = d // 2
    pad = CONV_WIDTH // 2
    off = 16

    hpad[:, 0:off, :] = jnp.zeros((rt, off, half), F32)
    hpad[:, off + w:off + w + off, :] = jnp.zeros((rt, off, half), F32)
    hpad[:, off:off + w, :] = cur_ref[0, :, :, 0:half]
    vbuf[0:rt] = jnp.where(i > 0, prev_ref[0], 0.0)
    vbuf[rt:2 * rt] = cur_ref[0, :, :, half:d]
    vbuf[2 * rt:3 * rt] = jnp.where(i < pl.num_programs(1) - 1, next_ref[0], 0.0)

    def row_body(r, carry):
        for lg in range(half // V7X_LANES):
            ls = slice(lg * V7X_LANES, (lg + 1) * V7X_LANES)
            ls_hi = slice(half + lg * V7X_LANES, half + (lg + 1) * V7X_LANES)
            acc_h = jnp.zeros((w, V7X_LANES), F32)
            acc_v = jnp.zeros((w, V7X_LANES), F32)
            for j in range(CONV_WIDTH):
                acc_h = acc_h + hpad[r, off - pad + j:off - pad + j + w, ls] * w_ref[j:j + 1, ls]
                acc_v = acc_v + vbuf[rt - pad + j + r, :, ls] * w_ref[j:j + 1, ls_hi]
            zbuf[r, :, ls] = acc_h
            zbuf[r, :, ls_hi] = acc_v
        return carry

    lax.fori_loop(0, rt, row_body, 0)

    z = zbuf[...].reshape(rt * w, d) + bdw_ref[...]
    mu = jnp.mean(z, axis=-1, keepdims=True)
    zc = z - mu
    y = zc * lax.rsqrt(jnp.mean(zc * zc, axis=-1, keepdims=True) + EPS) * gln_ref[...] + bln_ref[...]
    out_ref[0] = (y * jax.nn.sigmoid(y)).astype(BF16).reshape(rt, w, d)


def _dwconv(u, w_dw, b_dw, g_ln, b_ln, rt):
    b, r, w, d = u.shape
    half = d // 2
    nb = r // rt
    vec = pl.BlockSpec((1, d), lambda bi, i: (0, 0))
    return pl.pallas_call(
        _dwconv_kernel,
        grid=(b, nb),
        in_specs=[pl.BlockSpec((1, rt, w, d), lambda bi, i: (bi, i, 0, 0)),
                  pl.BlockSpec((1, rt, w, half), lambda bi, i: (bi, jnp.maximum(i - 1, 0), 0, 1)),
                  pl.BlockSpec((1, rt, w, half), lambda bi, i: (bi, jnp.minimum(i + 1, nb - 1), 0, 1)),
                  pl.BlockSpec((CONV_WIDTH, d), lambda bi, i: (0, 0)),
                  vec, vec, vec],
        out_specs=pl.BlockSpec((1, rt, w, d), lambda bi, i: (bi, i, 0, 0)),
        out_shape=jax.ShapeDtypeStruct((b, r, w, d), BF16),
        scratch_shapes=[pltpu.VMEM((rt, w + 32, half), F32),
                        pltpu.VMEM((3 * rt, w, half), F32),
                        pltpu.VMEM((rt, w, d), F32)],
        compiler_params=_params(("arbitrary", "arbitrary"), 48 << 20),
        name="dwconv_ln_silu",
    )(u, u, u, w_dw, b_dw, g_ln, b_ln)


def kernel(x, c, ctx, c_ctx, w_ada, b_ada, g_mix, g_ffn, w_hgrn_in, hgrn_gamma, g_hgrn_out, w_hgrn_out,
           w_cv_in, b_cv_in, w_cv_dw, b_cv_dw, g_cv_ln, b_cv_ln, w_cv_out, b_cv_out,
           w_router, b_router, w_exp_in, b_exp_in, w_exp_out, b_exp_out, g_final):
    bsz, seq, d = x.shape
    n = bsz * seq
    n_heads = d // HEAD_DIM
    n_exp = w_router.shape[-1]
    assert bsz + 1 <= 8 and seq % 512 == 0 and ctx.shape[1] % 256 == 0

    cond = jnp.zeros((8, d), F32).at[:bsz].set(c).at[bsz].set(c_ctx)
    mods = _ada_mods(cond, w_ada, b_ada).reshape(w_ada.shape[0], 8, 6, d)
    mods_lat = [mods[l, :bsz] for l in range(w_ada.shape[0])]
    mods_ctx = jnp.broadcast_to(mods[0, bsz:bsz + 1], (bsz, 6, d))

    lb_all = jnp.cumsum(jax.nn.softmax(hgrn_gamma.astype(F32), axis=1), axis=1)
    lb = lb_all[:, 0, :]

    row = lambda v: v.reshape(1, -1)
    wr_t = [w_router[l].T.astype(BF16) for l in range(2)]
    br = [jnp.broadcast_to(b_router[l][:, None], (n_exp, V7X_LANES)) for l in range(2)]

    w_in = w_hgrn_in[0].astype(BF16)
    scan_t = 256
    zeros_state = jnp.zeros((bsz, n_heads, HEAD_DIM, HEAD_DIM), F32)
    pc = _hgrn_proj(ctx, mods_ctx, row(g_mix[0]), lb, w_in, 256)
    _, _, s_f, s_b = _gla_scan(pc[0], pc[1], pc[2], pc[3], pc[4], pc[5], zeros_state, zeros_state, scan_t, 2)
    pq, pkf, pkb, pv, pbf, pcb, pg = _hgrn_proj(x, mods_lat[0], row(g_mix[0]), lb, w_in, 256)
    o_f, o_b, _, _ = _gla_scan(pq, pkf, pkb, pv, pbf, pcb, s_f, s_b, scan_t, 2)

    tm_tail = 512
    x1, hfp, route, gates, cnt = _hgrn_out(
        o_f.reshape(n, d), o_b.reshape(n, d), pg.reshape(n, d), x.reshape(n, d), mods_lat[0],
        row(g_hgrn_out[0]), w_hgrn_out[0].astype(BF16), row(g_ffn[0]), wr_t[0], br[0], tm_tail)

    tmb, tm_rows = 512, 256
    dest_tiles, ys = _moe_experts(route, cnt[:, 0].astype(I32), hfp, 0,
                                  w_exp_in, b_exp_in, w_exp_out, b_exp_out, tmb, tm_rows)

    x2, u = _combine_conv_in(dest_tiles, ys, gates.T, x1, mods_lat[0], mods_lat[1], row(g_mix[1]),
                             w_cv_in[0].astype(BF16), row(b_cv_in[0]), tm_rows)
    za = _dwconv(u.reshape(bsz, seq // GRID_W, GRID_W, d), w_cv_dw[0], row(b_cv_dw[0]),
                 row(g_cv_ln[0]), row(b_cv_ln[0]), 16)
    x3, hfp, route, gates, cnt = _conv_out(
        za.reshape(n, d), x2, mods_lat[1], w_cv_out[0].astype(BF16), row(b_cv_out[0]),
        row(g_ffn[1]), wr_t[1], br[1], tm_tail)
    dest_tiles, ys = _moe_experts(route, cnt[:, 0].astype(I32), hfp, 1,
                                  w_exp_in, b_exp_in, w_exp_out, b_exp_out, tmb, tm_rows)
    out = _combine_final(dest_tiles, ys, gates.T, x3, mods_lat[1], row(g_final), tm_rows)
    return out.reshape(bsz, seq, d)
```

```python
import functools

import jax
import jax.numpy as jnp
from jax import lax
from jax.experimental import pallas as pl
from jax.experimental.pallas import tpu as pltpu

F32 = jnp.float32
BF16 = jnp.bfloat16
I32 = jnp.int32
U32 = jnp.uint32

EPS = 1e-6
HEAD_DIM = 128
GLA_CHUNK = 64
GLA_DECAY_LIMIT = 60.0
DMA_ISSUE_UNROLL = 8
DMA_THREADS = 2
GRID_W = 64
CONV_WIDTH = 31
VCONV_PIECE = 16
HCONV_MARGIN = 16
HCONV_PITCH = 100
TOP_K = 4
SWIGLU_ALPHA = 1.702
SWIGLU_LIMIT = 7.0

V7X_LANES = 128
V7X_VMEM_BYTES = 64 * 1024 * 1024
VMEM_HEADROOM_BYTES = 8 * 1024 * 1024


def _params(semantics, vmem_bytes):
    limit = min(int(vmem_bytes), V7X_VMEM_BYTES - VMEM_HEADROOM_BYTES)
    return pltpu.CompilerParams(dimension_semantics=semantics, vmem_limit_bytes=limit)


def _dot(a, b):
    return jnp.dot(a, b, preferred_element_type=F32)


def _dot_nt(a, b):
    return lax.dot_general(a, b, (((1,), (1,)), ((), ())), preferred_element_type=F32)


def _dot_tn(a, b):
    return lax.dot_general(a, b, (((0,), (0,)), ((), ())), preferred_element_type=F32)


def _rmsnorm_mod(x, g, scale, shift):
    y = x * lax.rsqrt(jnp.mean(x * x, axis=-1, keepdims=True) + EPS)
    return y * g * (1.0 + scale) + shift


def _store_row_slabs(ref, x):
    m, d = x.shape
    slabs = d // V7X_LANES
    for s in range(slabs):
        ref[pl.ds(s, m, stride=slabs), :] = x[:, s * V7X_LANES:(s + 1) * V7X_LANES]


def _load_row_slabs(ref, slabs, *lead):
    m = ref.shape[-2] // slabs
    return jnp.concatenate([ref[(*lead, pl.ds(s, m, stride=slabs), slice(None))] for s in range(slabs)], axis=1)


def _slab(row, slabs):
    return pl.ds(pl.multiple_of(row * slabs, slabs), slabs)


def _ada_kernel(c_ref, w_ref, b_ref, o_ref):
    s = c_ref[...]
    s = s * jax.nn.sigmoid(s)
    o_ref[0] = jnp.dot(s, w_ref[0], preferred_element_type=F32,
                       precision=lax.Precision.HIGHEST) + b_ref[0]


def _ada_mods(cond, w_ada, b_ada):
    depth, d, d6 = w_ada.shape
    rows = cond.shape[0]
    tn = 1536
    return pl.pallas_call(
        _ada_kernel,
        grid=(depth, d6 // tn),
        in_specs=[pl.BlockSpec((rows, d), lambda l, n: (0, 0)),
                  pl.BlockSpec((1, d, tn), lambda l, n: (l, 0, n)),
                  pl.BlockSpec((1, 1, tn), lambda l, n: (l, 0, n))],
        out_specs=pl.BlockSpec((1, rows, tn), lambda l, n: (l, 0, n)),
        out_shape=jax.ShapeDtypeStruct((depth, rows, d6), F32),
        compiler_params=_params(("arbitrary", "arbitrary"), 24 << 20),
        name="ada_mods",
    )(cond, w_ada, b_ada.reshape(depth, 1, d6))


def _chunk_cumsum(x, chunk, reverse):
    rows, cols = x.shape
    pos = lax.broadcasted_iota(I32, (rows, V7X_LANES), 0) & (chunk - 1)
    step = 1
    while step < chunk:
        keep = (pos < chunk - step) if reverse else (pos >= step)
        shift = rows - step if reverse else step
        parts = []
        for c in range(cols // V7X_LANES):
            xs = x[:, c * V7X_LANES:(c + 1) * V7X_LANES]
            parts.append(xs + jnp.where(keep, pltpu.roll(xs, shift, 0), 0.0))
        x = jnp.concatenate(parts, axis=1)
        step *= 2
    return x


def _hgrn_proj_kernel(x_ref, mod_ref, gmix_ref, lb_ref, w_ref,
                      q_ref, kf_ref, kb_ref, v_ref, bf_ref, cb_ref, g_ref):
    d = x_ref.shape[-1]
    h = _rmsnorm_mod(x_ref[0], gmix_ref[...], mod_ref[0, 1:2, :], mod_ref[0, 0:1, :])
    p = _dot(h.astype(BF16), w_ref[...])
    q_ref[0] = p[:, 0:d].astype(BF16)
    v_ref[0] = p[:, d:2 * d].astype(BF16)
    g_ref[0] = p[:, 4 * d:5 * d].astype(BF16)
    lbf = lb_ref[0:1, :]
    lbb = lb_ref[1:2, :]
    ff = lbf + (1.0 - lbf) * jax.nn.sigmoid(p[:, 2 * d:3 * d])
    fb = lbb + (1.0 - lbb) * jax.nn.sigmoid(p[:, 3 * d:4 * d])
    kf_ref[0] = (1.0 - ff).astype(BF16)
    kb_ref[0] = (1.0 - fb).astype(BF16)
    bf_ref[0] = _chunk_cumsum(jnp.log(ff), GLA_CHUNK, reverse=False)
    cb_ref[0] = _chunk_cumsum(jnp.log(fb), GLA_CHUNK, reverse=True)


def _hgrn_proj(x, mods, g_mix, lb, w_in, tm):
    b, l, d = x.shape
    d5 = w_in.shape[1]
    tok = lambda bi, i: (bi, i, 0)
    const2 = lambda bi, i: (0, 0)
    bf_sd = jax.ShapeDtypeStruct((b, l, d), BF16)
    f_sd = jax.ShapeDtypeStruct((b, l, d), F32)
    return pl.pallas_call(
        _hgrn_proj_kernel,
        grid=(b, l // tm),
        in_specs=[pl.BlockSpec((1, tm, d), tok),
                  pl.BlockSpec((1, 6, d), lambda bi, i: (bi, 0, 0)),
                  pl.BlockSpec((1, d), const2),
                  pl.BlockSpec((2, d), const2),
                  pl.BlockSpec((d, d5), const2)],
        out_specs=[pl.BlockSpec((1, tm, d), tok)] * 7,
        out_shape=[bf_sd, bf_sd, bf_sd, bf_sd, f_sd, f_sd, bf_sd],
        compiler_params=_params(("arbitrary", "arbitrary"), 52 << 20),
        name="hgrn_proj",
    )(x, mods, g_mix, lb, w_in)


def _gla_chunk(q, k, v, cum, st, *, reverse, stable):
    c = q.shape[0]
    qf = q.astype(F32)
    kf = k.astype(F32)
    last = cum[0:1] if reverse else cum[c - 1:c]
    row = lax.broadcasted_iota(I32, (c, c), 0)
    col = lax.broadcasted_iota(I32, (c, c), 1)
    if not stable:
        mid_row = c // 2 if reverse else c // 2 - 1
        mid = cum[mid_row:mid_row + 1]
        x = cum - mid
        qe = qf * jnp.exp(x)
        ke = kf * jnp.exp(-x)
        a = _dot_nt(qe.astype(BF16), ke.astype(BF16))
        a = jnp.where((col >= row) if reverse else (row >= col), a, 0.0)
        qh = (qe * jnp.exp(mid)).astype(BF16)
        kh = (ke * jnp.exp(last - mid)).astype(BF16)
    else:
        a = jnp.where(row == col, _dot_nt(q, k), 0.0)
        c1 = cum.astype(BF16)
        r1 = cum - c1.astype(F32)
        c2 = r1.astype(BF16)
        c3 = (r1 - c2.astype(F32)).astype(BF16)
        level = 0
        while (1 << level) < c:
            m = 1 << level
            seg_row = row >> (level + 1)
            ref_row = (seg_row << (level + 1)) + (m if reverse else m - 1)
            sel = jnp.where(col == ref_row, 1.0, 0.0).astype(BF16)
            cref = _dot(sel, c1) + _dot(sel, c2) + _dot(sel, c3)
            e = jnp.exp(-jnp.abs(cum - cref))
            contrib = _dot_nt((qf * e).astype(BF16), (kf * e).astype(BF16))
            row_bit = (row >> level) & 1
            col_bit = (col >> level) & 1
            same = seg_row == (col >> (level + 1))
            if reverse:
                pick = jnp.where(same, (1 - row_bit) * col_bit, 0)
            else:
                pick = jnp.where(same, row_bit * (1 - col_bit), 0)
            a = a + jnp.where(pick == 1, contrib, 0.0)
            level += 1
        qh = (qf * jnp.exp(cum)).astype(BF16)
        kh = (kf * jnp.exp(last - cum)).astype(BF16)
    o = _dot(a.astype(BF16), v) + _dot_nt(qh, st.astype(BF16))
    st_new = st * jnp.exp(last) + _dot_tn(v, kh)
    return o, st_new


def _gla_scan_kernel(qf_ref, kf_ref, vf_ref, bf_ref, qb_ref, kb_ref, vb_ref, cb_ref, s0f_ref, s0b_ref,
                     of_ref, ob_ref, sf_ref, sb_ref):
    j = pl.program_id(2)

    @pl.when(j == 0)
    def _():
        sf_ref[...] = s0f_ref[...]
        sb_ref[...] = s0b_ref[...]

    t = qf_ref.shape[1]
    n_chunks = t // GLA_CHUNK
    n_heads = qf_ref.shape[2] // HEAD_DIM
    half = GLA_CHUNK // 2

    worst = jnp.zeros((1, qf_ref.shape[2]), F32)
    for c in range(n_chunks):
        lo = c * GLA_CHUNK
        bm = bf_ref[0, lo + half - 1:lo + half, :]
        bl = bf_ref[0, lo + GLA_CHUNK - 1:lo + GLA_CHUNK, :]
        cm = cb_ref[0, lo + half:lo + half + 1, :]
        c0 = cb_ref[0, lo:lo + 1, :]
        worst = jnp.maximum(jnp.maximum(worst, jnp.maximum(-bm, bm - bl)), jnp.maximum(-cm, cm - c0))
    steep = jnp.max(worst) > GLA_DECAY_LIMIT

    directions = ((qf_ref, kf_ref, vf_ref, bf_ref, of_ref, sf_ref, False),
                  (qb_ref, kb_ref, vb_ref, cb_ref, ob_ref, sb_ref, True))

    def chunk_step(refs, h, c, st, stable):
        q_ref, k_ref, v_ref, cum_ref, o_ref, _, reverse = refs
        ls = slice(h * HEAD_DIM, (h + 1) * HEAD_DIM)
        start = c * GLA_CHUNK
        rs = pl.ds(start if isinstance(c, int) else pl.multiple_of(start, GLA_CHUNK), GLA_CHUNK)
        o, st = _gla_chunk(q_ref[0, rs, ls], k_ref[0, rs, ls], v_ref[0, rs, ls], cum_ref[0, rs, ls],
                           st, reverse=reverse, stable=stable)
        o_ref[0, rs, ls] = o
        return st

    @pl.when(steep)
    def _():
        for refs in directions:
            for h in range(n_heads):
                def body(i, st, refs=refs, h=h):
                    c = n_chunks - 1 - i if refs[6] else i
                    return chunk_step(refs, h, c, st, True)

                refs[5][0, h] = lax.fori_loop(0, n_chunks, body, refs[5][0, h])

    @pl.when(jnp.logical_not(steep))
    def _():
        for h in range(n_heads):
            for refs in directions:
                st = refs[5][0, h]
                for i in range(n_chunks):
                    st = chunk_step(refs, h, n_chunks - 1 - i if refs[6] else i, st, False)
                refs[5][0, h] = st


def _gla_scan(q, kf, kb, v, bf, cb, s0f, s0b, t, heads_per_step):
    b, l, d = q.shape
    lanes = heads_per_step * HEAD_DIM
    nblk = l // t
    fwd = lambda bi, hp, j: (bi, j, hp)
    bwd = lambda bi, hp, j: (bi, nblk - 1 - j, hp)
    st_map = lambda bi, hp, j: (bi, hp, 0, 0)
    tok_f = pl.BlockSpec((1, t, lanes), fwd)
    tok_b = pl.BlockSpec((1, t, lanes), bwd)
    st_spec = pl.BlockSpec((1, heads_per_step, HEAD_DIM, HEAD_DIM), st_map)
    o_sd = jax.ShapeDtypeStruct((b, l, d), F32)
    s_sd = jax.ShapeDtypeStruct(s0f.shape, F32)
    return pl.pallas_call(
        _gla_scan_kernel,
        grid=(b, d // lanes, nblk),
        in_specs=[tok_f, tok_f, tok_f, tok_f, tok_b, tok_b, tok_b, tok_b, st_spec, st_spec],
        out_specs=[tok_f, tok_b, st_spec, st_spec],
        out_shape=[o_sd, o_sd, s_sd, s_sd],
        compiler_params=_params(("arbitrary", "arbitrary", "arbitrary"), 32 << 20),
        name="gla_scan",
    )(q, kf, v, bf, q, kb, v, cb, s0f, s0b)


def _ffn_route_tail(x, y, mod_ref, gffn_ref, wr_ref, br_ref, x1_ref, hfp_ref, route_ref, gate_ref, cnt_ref):
    tm = x.shape[0]
    x1 = x + mod_ref[0, 2:3, :] * y
    x1_ref[...] = x1
    hf = _rmsnorm_mod(x1, gffn_ref[...], mod_ref[0, 4:5, :], mod_ref[0, 3:4, :])
    _store_row_slabs(hfp_ref, hf)

    logits = _dot_nt(wr_ref[...], hf.astype(BF16)) + br_ref[:, 0:1]
    n_exp = logits.shape[0]
    eio = lax.broadcasted_iota(I32, (n_exp, tm), 0)
    vals, idxs, hots = [], [], []
    rest = logits
    for _ in range(TOP_K):
        m = jnp.max(rest, axis=0, keepdims=True)
        idx = jnp.min(jnp.where(rest == m, eio, n_exp), axis=0, keepdims=True)
        hot = eio == idx
        rest = jnp.where(hot, -jnp.inf, rest)
        vals.append(m)
        idxs.append(idx)
        hots.append(hot)
    ex = [jnp.exp(vk - vals[0]) for vk in vals]
    den = ex[0] + ex[1] + ex[2] + ex[3]
    gates = [e / den for e in ex]

    picked = jnp.zeros((n_exp, tm), F32)
    for hot in hots:
        picked = picked + jnp.where(hot, 1.0, 0.0)
    earlier = lax.broadcasted_iota(I32, (tm, tm), 0) < lax.broadcasted_iota(I32, (tm, tm), 1)
    before = _dot(picked.astype(BF16), jnp.where(earlier, 1.0, 0.0).astype(BF16)) + cnt_ref[:, 0:1]
    ranks = [jnp.sum(jnp.where(hot, before, 0.0), axis=0, keepdims=True).astype(I32) for hot in hots]
    cnt_ref[...] = cnt_ref[...] + jnp.sum(picked, axis=1, keepdims=True)

    route_ref[...] = jnp.concatenate(idxs + ranks, axis=0)
    gate_ref[...] = jnp.concatenate(gates + [jnp.zeros_like(g) for g in gates], axis=0)


def _hgrn_out_kernel(of_ref, ob_ref, g_ref, x_ref, mod_ref, gout_ref, wout_ref, gffn_ref, wr_ref, br_ref,
                     x1_ref, hfp_ref, route_ref, gate_ref, cnt_ref):
    @pl.when(pl.program_id(0) == 0)
    def _():
        cnt_ref[...] = jnp.zeros_like(cnt_ref)

    o = of_ref[...] + ob_ref[...]
    parts = []
    for h in range(o.shape[1] // HEAD_DIM):
        oh = o[:, h * HEAD_DIM:(h + 1) * HEAD_DIM]
        parts.append(oh * lax.rsqrt(jnp.mean(oh * oh, axis=-1, keepdims=True) + EPS))
    on = jnp.concatenate(parts, axis=1) * gout_ref[...]
    g = g_ref[...].astype(F32)
    a = on * (g * jax.nn.sigmoid(g))
    y = _dot(a.astype(BF16), wout_ref[...])
    _ffn_route_tail(x_ref[...], y, mod_ref, gffn_ref, wr_ref, br_ref,
                    x1_ref, hfp_ref, route_ref, gate_ref, cnt_ref)


def _conv_out_kernel(a_ref, x_ref, mod_ref, wout_ref, bout_ref, gffn_ref, wr_ref, br_ref,
                     x1_ref, hfp_ref, route_ref, gate_ref, cnt_ref):
    @pl.when(pl.program_id(0) == 0)
    def _():
        cnt_ref[...] = jnp.zeros_like(cnt_ref)

    y = _dot(a_ref[...], wout_ref[...]) + bout_ref[...]
    _ffn_route_tail(x_ref[...], y, mod_ref, gffn_ref, wr_ref, br_ref,
                    x1_ref, hfp_ref, route_ref, gate_ref, cnt_ref)


def _tail_specs(n, d, n_exp, tm, tiles_per_batch):
    tok = lambda i: (i, 0)
    const2 = lambda i: (0, 0)
    specs = dict(
        tok=pl.BlockSpec((tm, d), tok),
        mod=pl.BlockSpec((1, 6, d), lambda i: (i // tiles_per_batch, 0, 0)),
        vec=pl.BlockSpec((1, d), const2),
        mat=pl.BlockSpec((d, d), const2),
        wr=pl.BlockSpec((n_exp, d), const2),
        br=pl.BlockSpec((n_exp, V7X_LANES), const2),
    )
    out_specs = [pl.BlockSpec((tm, d), tok),
                 pl.BlockSpec((tm * (d // V7X_LANES), V7X_LANES), tok),
                 pl.BlockSpec((8, tm), lambda i: (0, i)),
                 pl.BlockSpec((8, tm), lambda i: (0, i)),
                 pl.BlockSpec((n_exp, V7X_LANES), const2)]
    out_shape = [jax.ShapeDtypeStruct((n, d), F32),
                 jax.ShapeDtypeStruct((n * (d // V7X_LANES), V7X_LANES), F32),
                 jax.ShapeDtypeStruct((8, n), I32),
                 jax.ShapeDtypeStruct((8, n), F32),
                 jax.ShapeDtypeStruct((n_exp, V7X_LANES), F32)]
    return specs, out_specs, out_shape


def _hgrn_out(o_f, o_b, g, x, mods, g_out, w_out, g_ffn, wr_t, br, tm):
    n, d = x.shape
    s, out_specs, out_shape = _tail_specs(n, d, wr_t.shape[0], tm, n // mods.shape[0] // tm)
    return pl.pallas_call(
        _hgrn_out_kernel,
        grid=(n // tm,),
        in_specs=[s["tok"], s["tok"], s["tok"], s["tok"], s["mod"], s["vec"], s["mat"], s["vec"], s["wr"], s["br"]],
        out_specs=out_specs,
        out_shape=out_shape,
        compiler_params=_params(("arbitrary",), 48 << 20),
        name="hgrn_out_route",
    )(o_f, o_b, g, x, mods, g_out, w_out, g_ffn, wr_t, br)


def _conv_out(a, x, mods, w_out, b_out, g_ffn, wr_t, br, tm):
    n, d = x.shape
    s, out_specs, out_shape = _tail_specs(n, d, wr_t.shape[0], tm, n // mods.shape[0] // tm)
    return pl.pallas_call(
        _conv_out_kernel,
        grid=(n // tm,),
        in_specs=[s["tok"], s["tok"], s["mod"], s["mat"], s["vec"], s["vec"], s["wr"], s["br"]],
        out_specs=out_specs,
        out_shape=out_shape,
        compiler_params=_params(("arbitrary",), 40 << 20),
        name="conv_out_route",
    )(a, x, mods, w_out, b_out, g_ffn, wr_t, br)


def _routing_plan(route, counts, n_blocks, tmb, tm):
    n_exp = counts.shape[0]
    padded = (counts + tmb - 1) // tmb * tmb
    pend = jnp.cumsum(padded)
    pstart = pend - padded
    n_valid = pend[-1] // tmb
    blk = jnp.arange(n_blocks, dtype=I32)
    blk_e = jnp.minimum(jnp.sum((blk[:, None] * tmb >= pend[None, :]).astype(I32), axis=1), n_exp - 1)
    blk_e = jnp.where(blk < n_valid, blk_e, blk_e[jnp.maximum(n_valid - 1, 0)])
    eidx, rank = route[:TOP_K], route[TOP_K:2 * TOP_K]
    hit = eidx[None] == jnp.arange(n_exp, dtype=I32)[:, None, None]
    dest = rank + jnp.sum(jnp.where(hit, pstart[:, None, None], 0), axis=0)
    n = dest.shape[1]
    dest_tiles = dest.reshape(TOP_K, n // tm, tm).transpose(1, 0, 2).reshape(-1)
    return dest_tiles.astype(I32), blk_e, n_valid.reshape(1).astype(I32), (pstart + counts).astype(I32)


def _dispatch_kernel(fill_ref, n_valid_ref, dest_ref, hfp_ref, xs_ref, zero_buf, sem, *, slabs):
    tm = hfp_ref.shape[0] // slabs
    fill_rows = zero_buf.shape[0] // slabs

    @pl.when(pl.program_id(0) == 0)
    def _():
        zero_buf[...] = jnp.zeros_like(zero_buf)

        def pad_rows(start):
            return xs_ref.at[pl.ds(pl.multiple_of(start * slabs, slabs), fill_rows * slabs)]

        for e in range(fill_ref.shape[0]):
            pltpu.make_async_copy(zero_buf, pad_rows(fill_ref[e]), sem).start()
        for e in range(fill_ref.shape[0]):
            pltpu.make_async_copy(zero_buf, pad_rows(fill_ref[e]), sem).wait()

        def clear(b, carry):
            tail = pltpu.make_async_copy(zero_buf, pad_rows(b * fill_rows), sem)
            tail.start()
            tail.wait()
            return carry

        lax.fori_loop(n_valid_ref[0], xs_ref.shape[0] // (fill_rows * slabs), clear, 0)

    def issue(r, carry):
        for k in range(TOP_K):
            pltpu.make_async_copy(hfp_ref.at[_slab(r, slabs)], xs_ref.at[_slab(dest_ref[k * tm + r], slabs)],
                                  sem).start(priority=k % DMA_THREADS)
        return carry

    lax.fori_loop(0, tm, issue, 0, unroll=DMA_ISSUE_UNROLL)
    for k in range(TOP_K):
        pltpu.make_async_copy(hfp_ref, xs_ref.at[pl.ds(0, tm * slabs)], sem).wait()


def _dispatch(fill, n_valid, dest_tiles, hfp, slabs, rows, fill_rows, tm):
    lanes = hfp.shape[1]
    return pl.pallas_call(
        functools.partial(_dispatch_kernel, slabs=slabs),
        grid_spec=pltpu.PrefetchScalarGridSpec(
            num_scalar_prefetch=2,
            grid=(hfp.shape[0] // (tm * slabs),),
            in_specs=[pl.BlockSpec((TOP_K * tm,), lambda i, fill, nv: (i,), memory_space=pltpu.SMEM),
                      pl.BlockSpec((tm * slabs, lanes), lambda i, fill, nv: (i, 0))],
            out_specs=pl.BlockSpec(memory_space=pl.ANY),
            scratch_shapes=[pltpu.VMEM((fill_rows * slabs, lanes), hfp.dtype), pltpu.SemaphoreType.DMA(())]),
        out_shape=jax.ShapeDtypeStruct((rows * slabs, lanes), hfp.dtype),
        compiler_params=_params(("arbitrary",), 16 << 20),
        name="moe_dispatch",
    )(fill, n_valid, dest_tiles, hfp)


def _expert_kernel(blk_e_ref, n_valid_ref, xs_ref, w1_ref, b1_ref, w2_ref, b2_ref, ys_ref, w1_bf, w2_bf):
    i = pl.program_id(0)
    live = i < n_valid_ref[0]
    new_expert = jnp.logical_or(i == 0, blk_e_ref[i] != blk_e_ref[jnp.maximum(i - 1, 0)])

    @pl.when(jnp.logical_and(live, new_expert))
    def _():
        w1_bf[...] = w1_ref[0, 0].astype(BF16)
        w2_bf[...] = w2_ref[0, 0].astype(BF16)

    @pl.when(live)
    def _():
        slabs = w1_bf.shape[0] // V7X_LANES
        h = _dot(_load_row_slabs(xs_ref, slabs).astype(BF16), w1_bf[...]) + b1_ref[0, 0]
        de = h.shape[1] // 2
        gate = jnp.minimum(h[:, :de], SWIGLU_LIMIT)
        up = jnp.clip(h[:, de:], -SWIGLU_LIMIT, SWIGLU_LIMIT)
        act = (up + 1.0) * gate * jax.nn.sigmoid(SWIGLU_ALPHA * gate)
        _store_row_slabs(ys_ref, _dot(act.astype(BF16), w2_bf[...]) + b2_ref[0, 0])

    @pl.when(jnp.logical_not(live))
    def _():
        ys_ref[...] = jnp.zeros_like(ys_ref)


def _experts(blk_e, n_valid, xs, layer, w1, b1, w2, b2, n_blocks, tmb):
    _, n_exp, d, de2 = w1.shape
    slabs = d // V7X_LANES
    row_map = lambda i, be, nv: (jnp.minimum(i, nv[0] - 1), 0)
    w_map = lambda i, be, nv: (layer, be[i], 0, 0)
    return pl.pallas_call(
        _expert_kernel,
        grid_spec=pltpu.PrefetchScalarGridSpec(
            num_scalar_prefetch=2,
            grid=(n_blocks,),
            in_specs=[pl.BlockSpec((tmb * slabs, V7X_LANES), row_map),
                      pl.BlockSpec((1, 1, d, de2), w_map),
                      pl.BlockSpec((1, 1, 1, de2), w_map),
                      pl.BlockSpec((1, 1, de2 // 2, d), w_map),
                      pl.BlockSpec((1, 1, 1, d), w_map)],
            out_specs=pl.BlockSpec((tmb * slabs, V7X_LANES), lambda i, be, nv: (i, 0)),
            scratch_shapes=[pltpu.VMEM((d, de2), BF16), pltpu.VMEM((de2 // 2, d), BF16)]),
        out_shape=jax.ShapeDtypeStruct((n_blocks * tmb * slabs, V7X_LANES), xs.dtype),
        compiler_params=_params(("arbitrary",), 56 << 20),
        name="moe_experts",
    )(blk_e, n_valid, xs, w1, b1.reshape(-1, n_exp, 1, de2), w2, b2.reshape(-1, n_exp, 1, d))


def _gather_start(dest_ref, ys_ref, buf, sem, slot, tm, slabs):
    def issue(r, carry):
        for k in range(TOP_K):
            pltpu.make_async_copy(ys_ref.at[_slab(dest_ref[k * tm + r], slabs)],
                                  buf.at[slot, k, _slab(r, slabs)], sem.at[slot]).start(priority=k % DMA_THREADS)
        return carry

    lax.fori_loop(0, tm, issue, 0, unroll=DMA_ISSUE_UNROLL)


def _gather_combine(dest_ref, dest_next_ref, ys_ref, gate_ref, buf, sem):
    i = pl.program_id(0)
    slot = lax.rem(i, 2)
    tm = gate_ref.shape[0]
    slabs = buf.shape[2] // tm

    @pl.when(i == 0)
    def _():
        _gather_start(dest_ref, ys_ref, buf, sem, 0, tm, slabs)

    @pl.when(i + 1 < pl.num_programs(0))
    def _():
        _gather_start(dest_next_ref, ys_ref, buf, sem, 1 - slot, tm, slabs)

    for k in range(TOP_K):
        pltpu.make_async_copy(ys_ref.at[pl.ds(0, tm * slabs)], buf.at[slot, k], sem.at[slot]).wait()
    y = None
    for k in range(TOP_K):
        rows = _load_row_slabs(buf, slabs, slot, k)
        y = gate_ref[:, k:k + 1] * rows if y is None else y + gate_ref[:, k:k + 1] * rows
    return y


def _combine_conv_in_kernel(dest_ref, dest_next_ref, ys_ref, gate_ref, x_ref, mod_ref, modn_ref, gmix_ref,
                            win_ref, bin_ref, x2_ref, u_ref, buf, sem):
    y = _gather_combine(dest_ref, dest_next_ref, ys_ref, gate_ref, buf, sem)
    x2 = x_ref[...] + mod_ref[0, 5:6, :] * y
    x2_ref[...] = x2
    h = _rmsnorm_mod(x2, gmix_ref[...], modn_ref[0, 1:2, :], modn_ref[0, 0:1, :])
    p = _dot(h.astype(BF16), win_ref[...]) + bin_ref[...]
    d = x2.shape[1]
    u_ref[...] = p[:, :d] * jax.nn.sigmoid(p[:, d:])


def _combine_final_kernel(dest_ref, dest_next_ref, ys_ref, gate_ref, x_ref, mod_ref, gfin_ref, out_ref, buf, sem):
    y = _gather_combine(dest_ref, dest_next_ref, ys_ref, gate_ref, buf, sem)
    x4 = x_ref[...] + mod_ref[0, 5:6, :] * y
    out_ref[...] = x4 * lax.rsqrt(jnp.mean(x4 * x4, axis=-1, keepdims=True) + EPS) * gfin_ref[...]


def _combine_common(n, d, tm, tiles_per_batch):
    tok = lambda i: (i, 0)
    last = n // tm - 1
    return dict(
        dest=pl.BlockSpec((TOP_K * tm,), lambda i: (i,), memory_space=pltpu.SMEM),
        dest_next=pl.BlockSpec((TOP_K * tm,), lambda i: (jnp.minimum(i + 1, last),), memory_space=pltpu.SMEM),
        ys=pl.BlockSpec(memory_space=pl.ANY),
        gate=pl.BlockSpec((tm, 8), tok),
        tok=pl.BlockSpec((tm, d), tok),
        mod=pl.BlockSpec((1, 6, d), lambda i: (i // tiles_per_batch, 0, 0)),
        vec=pl.BlockSpec((1, d), lambda i: (0, 0)),
        scratch=[pltpu.VMEM((2, TOP_K, tm * (d // V7X_LANES), V7X_LANES), F32), pltpu.SemaphoreType.DMA((2,))],
    )


def _combine_conv_in(dest_tiles, ys, gates, x, mods, mods_next, g_mix, w_in, b_in, tm):
    n, d = x.shape
    s = _combine_common(n, d, tm, n // mods.shape[0] // tm)
    return pl.pallas_call(
        _combine_conv_in_kernel,
        grid=(n // tm,),
        in_specs=[s["dest"], s["dest_next"], s["ys"], s["gate"], s["tok"], s["mod"], s["mod"], s["vec"],
                  pl.BlockSpec((d, 2 * d), lambda i: (0, 0)),
                  pl.BlockSpec((1, 2 * d), lambda i: (0, 0))],
        out_specs=[s["tok"], s["tok"]],
        out_shape=[jax.ShapeDtypeStruct((n, d), F32), jax.ShapeDtypeStruct((n, d), F32)],
        scratch_shapes=s["scratch"],
        compiler_params=_params(("arbitrary",), 48 << 20),
        name="moe_combine_conv_in",
    )(dest_tiles, dest_tiles, ys, gates, x, mods, mods_next, g_mix, w_in, b_in)


def _combine_final(dest_tiles, ys, gates, x, mods, g_final, tm):
    n, d = x.shape
    s = _combine_common(n, d, tm, n // mods.shape[0] // tm)
    return pl.pallas_call(
        _combine_final_kernel,
        grid=(n // tm,),
        in_specs=[s["dest"], s["dest_next"], s["ys"], s["gate"], s["tok"], s["mod"], s["vec"]],
        out_specs=s["tok"],
        out_shape=jax.ShapeDtypeStruct((n, d), F32),
        scratch_shapes=s["scratch"],
        compiler_params=_params(("arbitrary",), 32 << 20),
        name="moe_combine_final",
    )(dest_tiles, dest_tiles, ys, gates, x, mods, g_final)


def _moe_experts(route, counts, hfp, layer, w1, b1, w2, b2, tmb, tm):
    slabs = w1.shape[2] // V7X_LANES
    n = hfp.shape[0] // slabs
    n_exp = w1.shape[1]
    n_blocks = (n * TOP_K + n_exp * (tmb - 1) + tmb - 1) // tmb
    dest_tiles, blk_e, n_valid, fill = _routing_plan(route, counts, n_blocks, tmb, tm)
    xs = _dispatch(fill, n_valid, dest_tiles, hfp, slabs, (n_blocks + 1) * tmb, tmb, tm)
    ys = _experts(blk_e, n_valid, xs, layer, w1, b1, w2, b2, n_blocks, tmb)
    return dest_tiles, ys


def _dwconv_kernel(cur_ref, prev_ref, next_ref, w_ref, bdw_ref, gln_ref, bln_ref, out_ref, hbuf, vbuf, zbuf):
    i = pl.program_id(1)
    rt, w, d = cur_ref.shape[1], cur_ref.shape[2], cur_ref.shape[3]
    half = d // 2
    pad = CONV_WIDTH // 2
    n_lg = half // V7X_LANES

    for r in range(rt):
        base = r * HCONV_PITCH
        for lg in range(n_lg):
            hbuf[lg, base:base + HCONV_MARGIN, :] = jnp.zeros((HCONV_MARGIN, V7X_LANES), F32)
            hbuf[lg, base + HCONV_MARGIN:base + HCONV_MARGIN + w, :] = cur_ref[0, r, :, lg * V7X_LANES:(lg + 1) * V7X_LANES]
            hbuf[lg, base + HCONV_MARGIN + w:base + HCONV_PITCH, :] = jnp.zeros(
                (HCONV_PITCH - HCONV_MARGIN - w, V7X_LANES), F32)
    vbuf[0:rt] = jnp.where(i > 0, prev_ref[0], 0.0)
    vbuf[rt:2 * rt] = cur_ref[0, :, :, half:d]
    vbuf[2 * rt:3 * rt] = jnp.where(i < pl.num_programs(1) - 1, next_ref[0], 0.0)

    def col_body(wi, carry):
        for lg in range(n_lg):
            ls = slice(lg * V7X_LANES, (lg + 1) * V7X_LANES)
            acc = jnp.zeros((rt, V7X_LANES), F32)
            for j in range(CONV_WIDTH):
                taps = hbuf[lg, pl.ds(wi + (HCONV_MARGIN - pad + j), rt, stride=HCONV_PITCH), :]
                acc = acc + taps * w_ref[j:j + 1, ls]
            zbuf[lg, pl.ds(wi, rt, stride=w), :] = acc
        return carry

    lax.fori_loop(0, w, col_body, 0)

    def row_body(r, carry):
        def piece_body(p, carry_p):
            w0 = pl.multiple_of(p * VCONV_PIECE, VCONV_PIECE)
            for lg in range(n_lg):
                ls = slice(lg * V7X_LANES, (lg + 1) * V7X_LANES)
                ls_hi = slice(half + lg * V7X_LANES, half + (lg + 1) * V7X_LANES)
                acc = jnp.zeros((VCONV_PIECE, V7X_LANES), F32)
                for j in range(CONV_WIDTH):
                    acc = acc + vbuf[rt - pad + j + r, pl.ds(w0, VCONV_PIECE), ls] * w_ref[j:j + 1, ls_hi]
                zbuf[n_lg + lg, pl.ds(pl.multiple_of(r * w + w0, VCONV_PIECE), VCONV_PIECE), :] = acc
            return carry_p

        return lax.fori_loop(0, w // VCONV_PIECE, piece_body, carry)

    lax.fori_loop(0, rt, row_body, 0)

    z = jnp.concatenate([zbuf[s] for s in range(d // V7X_LANES)], axis=1) + bdw_ref[...]
    mu = jnp.mean(z, axis=-1, keepdims=True)
    zc = z - mu
    y = zc * lax.rsqrt(jnp.mean(zc * zc, axis=-1, keepdims=True) + EPS) * gln_ref[...] + bln_ref[...]
    out_ref[0] = (y * jax.nn.sigmoid(y)).astype(BF16).reshape(rt, w, d)


def _dwconv(u, w_dw, b_dw, g_ln, b_ln, rt):
    b, r, w, d = u.shape
    half = d // 2
    nb = r // rt
    vec = pl.BlockSpec((1, d), lambda bi, i: (0, 0))
    return pl.pallas_call(
        _dwconv_kernel,
        grid=(b, nb),
        in_specs=[pl.BlockSpec((1, rt, w, d), lambda bi, i: (bi, i, 0, 0)),
                  pl.BlockSpec((1, rt, w, half), lambda bi, i: (bi, jnp.maximum(i - 1, 0), 0, 1)),
                  pl.BlockSpec((1, rt, w, half), lambda bi, i: (bi, jnp.minimum(i + 1, nb - 1), 0, 1)),
                  pl.BlockSpec((CONV_WIDTH, d), lambda bi, i: (0, 0)),
                  vec, vec, vec],
        out_specs=pl.BlockSpec((1, rt, w, d), lambda bi, i: (bi, i, 0, 0)),
        out_shape=jax.ShapeDtypeStruct((b, r, w, d), BF16),
        scratch_shapes=[pltpu.VMEM((half // V7X_LANES, rt * HCONV_PITCH, V7X_LANES), F32),
                        pltpu.VMEM((3 * rt, w, half), F32),
                        pltpu.VMEM((d // V7X_LANES, rt * w, V7X_LANES), F32)],
        compiler_params=_params(("arbitrary", "arbitrary"), 48 << 20),
        name="dwconv_ln_silu",
    )(u, u, u, w_dw, b_dw, g_ln, b_ln)


def kernel(x, c, ctx, c_ctx, w_ada, b_ada, g_mix, g_ffn, w_hgrn_in, hgrn_gamma, g_hgrn_out, w_hgrn_out,
           w_cv_in, b_cv_in, w_cv_dw, b_cv_dw, g_cv_ln, b_cv_ln, w_cv_out, b_cv_out,
           w_router, b_router, w_exp_in, b_exp_in, w_exp_out, b_exp_out, g_final):
    bsz, seq, d = x.shape
    n = bsz * seq
    n_heads = d // HEAD_DIM
    n_exp = w_router.shape[-1]
    assert bsz + 1 <= 8 and seq % 512 == 0 and ctx.shape[1] % 256 == 0

    cond = jnp.zeros((8, d), F32).at[:bsz].set(c).at[bsz].set(c_ctx)
    mods = _ada_mods(cond, w_ada, b_ada).reshape(w_ada.shape[0], 8, 6, d)
    mods_lat = [mods[l, :bsz] for l in range(w_ada.shape[0])]
    mods_ctx = jnp.broadcast_to(mods[0, bsz:bsz + 1], (bsz, 6, d))

    lb_all = jnp.cumsum(jax.nn.softmax(hgrn_gamma.astype(F32), axis=1), axis=1)
    lb = lb_all[:, 0, :]

    row = lambda v: v.reshape(1, -1)
    wr_t = [w_router[l].T.astype(BF16) for l in range(2)]
    br = [jnp.broadcast_to(b_router[l][:, None], (n_exp, V7X_LANES)) for l in range(2)]

    w_in = w_hgrn_in[0].astype(BF16)
    scan_t = 256
    zeros_state = jnp.zeros((bsz, n_heads, HEAD_DIM, HEAD_DIM), F32)
    pc = _hgrn_proj(ctx, mods_ctx, row(g_mix[0]), lb, w_in, 256)
    _, _, s_f, s_b = _gla_scan(pc[0], pc[1], pc[2], pc[3], pc[4], pc[5], zeros_state, zeros_state, scan_t, n_heads)
    pq, pkf, pkb, pv, pbf, pcb, pg = _hgrn_proj(x, mods_lat[0], row(g_mix[0]), lb, w_in, 256)
    o_f, o_b, _, _ = _gla_scan(pq, pkf, pkb, pv, pbf, pcb, s_f, s_b, scan_t, n_heads)

    tm_tail = 512
    x1, hfp, route, gates, cnt = _hgrn_out(
        o_f.reshape(n, d), o_b.reshape(n, d), pg.reshape(n, d), x.reshape(n, d), mods_lat[0],
        row(g_hgrn_out[0]), w_hgrn_out[0].astype(BF16), row(g_ffn[0]), wr_t[0], br[0], tm_tail)

    tmb, tm_rows = 512, 256
    dest_tiles, ys = _moe_experts(route, cnt[:, 0].astype(I32), hfp, 0,
                                  w_exp_in, b_exp_in, w_exp_out, b_exp_out, tmb, tm_rows)

    x2, u = _combine_conv_in(dest_tiles, ys, gates.T, x1, mods_lat[0], mods_lat[1], row(g_mix[1]),
                             w_cv_in[0].astype(BF16), row(b_cv_in[0]), tm_rows)
    za = _dwconv(u.reshape(bsz, seq // GRID_W, GRID_W, d), w_cv_dw[0], row(b_cv_dw[0]),
                 row(g_cv_ln[0]), row(b_cv_ln[0]), 16)
    x3, hfp, route, gates, cnt = _conv_out(
        za.reshape(n, d), x2, mods_lat[1], w_cv_out[0].astype(BF16), row(b_cv_out[0]),
        row(g_ffn[1]), wr_t[1], br[1], tm_tail)
    dest_tiles, ys = _moe_experts(route, cnt[:, 0].astype(I32), hfp, 1,
                                  w_exp_in, b_exp_in, w_exp_out, b_exp_out, tmb, tm_rows)
    out = _combine_final(dest_tiles, ys, gates.T, x3, mods_lat[1], row(g_final), tm_rows)
    return out.reshape(bsz, seq, d)
```

```python
import functools

import jax
import jax.numpy as jnp
from jax import lax
from jax.experimental import pallas as pl
from jax.experimental.pallas import tpu as pltpu

F32 = jnp.float32
BF16 = jnp.bfloat16
I32 = jnp.int32
U32 = jnp.uint32

EPS = 1e-6
HEAD_DIM = 128
GLA_CHUNK = 64
GLA_DECAY_LIMIT = 60.0
DMA_ISSUE_UNROLL = 8
DMA_THREADS = 2
GRID_W = 64
CONV_WIDTH = 31
VCONV_PIECE = 16
HCONV_MARGIN = 16
HCONV_PITCH = 100
TOP_K = 4
SWIGLU_ALPHA = 1.702
SWIGLU_LIMIT = 7.0

V7X_LANES = 128
V7X_VMEM_BYTES = 64 * 1024 * 1024
VMEM_HEADROOM_BYTES = 8 * 1024 * 1024


def _params(semantics, vmem_bytes):
    limit = min(int(vmem_bytes), V7X_VMEM_BYTES - VMEM_HEADROOM_BYTES)
    return pltpu.CompilerParams(dimension_semantics=semantics, vmem_limit_bytes=limit)


def _dot(a, b):
    return jnp.dot(a, b, preferred_element_type=F32)


def _dot_nt(a, b):
    return lax.dot_general(a, b, (((1,), (1,)), ((), ())), preferred_element_type=F32)


def _dot_tn(a, b):
    return lax.dot_general(a, b, (((0,), (0,)), ((), ())), preferred_element_type=F32)


def _rmsnorm_mod(x, g, scale, shift):
    y = x * lax.rsqrt(jnp.mean(x * x, axis=-1, keepdims=True) + EPS)
    return y * g * (1.0 + scale) + shift


def _store_row_slabs(ref, x, *lead):
    m, d = x.shape
    slabs = d // V7X_LANES
    for s in range(slabs):
        ref[(*lead, pl.ds(s, m, stride=slabs), slice(None))] = x[:, s * V7X_LANES:(s + 1) * V7X_LANES]


def _load_row_slabs(ref, slabs, *lead):
    m = ref.shape[-2] // slabs
    return jnp.concatenate([ref[(*lead, pl.ds(s, m, stride=slabs), slice(None))] for s in range(slabs)], axis=1)


def _slab(row, slabs):
    return pl.ds(pl.multiple_of(row * slabs, slabs), slabs)


def _ada_kernel(c_ref, w_ref, b_ref, o_ref):
    s = c_ref[...]
    s = s * jax.nn.sigmoid(s)
    o_ref[0] = jnp.dot(s, w_ref[0], preferred_element_type=F32,
                       precision=lax.Precision.HIGHEST) + b_ref[0]


def _ada_mods(cond, w_ada, b_ada):
    depth, d, d6 = w_ada.shape
    rows = cond.shape[0]
    tn = 1536
    return pl.pallas_call(
        _ada_kernel,
        grid=(depth, d6 // tn),
        in_specs=[pl.BlockSpec((rows, d), lambda l, n: (0, 0)),
                  pl.BlockSpec((1, d, tn), lambda l, n: (l, 0, n)),
                  pl.BlockSpec((1, 1, tn), lambda l, n: (l, 0, n))],
        out_specs=pl.BlockSpec((1, rows, tn), lambda l, n: (l, 0, n)),
        out_shape=jax.ShapeDtypeStruct((depth, rows, d6), F32),
        compiler_params=_params(("arbitrary", "arbitrary"), 24 << 20),
        name="ada_mods",
    )(cond, w_ada, b_ada.reshape(depth, 1, d6))


def _chunk_cumsum(x, chunk, reverse):
    rows, cols = x.shape
    pos = lax.broadcasted_iota(I32, (rows, V7X_LANES), 0) & (chunk - 1)
    step = 1
    while step < chunk:
        keep = (pos < chunk - step) if reverse else (pos >= step)
        shift = rows - step if reverse else step
        parts = []
        for c in range(cols // V7X_LANES):
            xs = x[:, c * V7X_LANES:(c + 1) * V7X_LANES]
            parts.append(xs + jnp.where(keep, pltpu.roll(xs, shift, 0), 0.0))
        x = jnp.concatenate(parts, axis=1)
        step *= 2
    return x


def _hgrn_proj_kernel(x_ref, mod_ref, gmix_ref, lb_ref, w_ref,
                      q_ref, kf_ref, kb_ref, v_ref, bf_ref, cb_ref, g_ref):
    d = x_ref.shape[-1]
    h = _rmsnorm_mod(x_ref[0], gmix_ref[...], mod_ref[0, 1:2, :], mod_ref[0, 0:1, :])
    p = _dot(h.astype(BF16), w_ref[...])
    q_ref[0] = p[:, 0:d].astype(BF16)
    v_ref[0] = p[:, d:2 * d].astype(BF16)
    g_ref[0] = p[:, 4 * d:5 * d].astype(BF16)
    lbf = lb_ref[0:1, :]
    lbb = lb_ref[1:2, :]
    ff = lbf + (1.0 - lbf) * jax.nn.sigmoid(p[:, 2 * d:3 * d])
    fb = lbb + (1.0 - lbb) * jax.nn.sigmoid(p[:, 3 * d:4 * d])
    kf_ref[0] = (1.0 - ff).astype(BF16)
    kb_ref[0] = (1.0 - fb).astype(BF16)
    bf_ref[0] = _chunk_cumsum(jnp.log(ff), GLA_CHUNK, reverse=False)
    cb_ref[0] = _chunk_cumsum(jnp.log(fb), GLA_CHUNK, reverse=True)


def _hgrn_proj(x, mods, g_mix, lb, w_in, tm):
    b, l, d = x.shape
    d5 = w_in.shape[1]
    tok = lambda bi, i: (bi, i, 0)
    const2 = lambda bi, i: (0, 0)
    bf_sd = jax.ShapeDtypeStruct((b, l, d), BF16)
    f_sd = jax.ShapeDtypeStruct((b, l, d), F32)
    return pl.pallas_call(
        _hgrn_proj_kernel,
        grid=(b, l // tm),
        in_specs=[pl.BlockSpec((1, tm, d), tok),
                  pl.BlockSpec((1, 6, d), lambda bi, i: (bi, 0, 0)),
                  pl.BlockSpec((1, d), const2),
                  pl.BlockSpec((2, d), const2),
                  pl.BlockSpec((d, d5), const2)],
        out_specs=[pl.BlockSpec((1, tm, d), tok)] * 7,
        out_shape=[bf_sd, bf_sd, bf_sd, bf_sd, f_sd, f_sd, bf_sd],
        compiler_params=_params(("arbitrary", "arbitrary"), 52 << 20),
        name="hgrn_proj",
    )(x, mods, g_mix, lb, w_in)


def _gla_chunk(q, k, v, cum, st, *, reverse, stable):
    c = q.shape[0]
    qf = q.astype(F32)
    kf = k.astype(F32)
    last = cum[0:1] if reverse else cum[c - 1:c]
    row = lax.broadcasted_iota(I32, (c, c), 0)
    col = lax.broadcasted_iota(I32, (c, c), 1)
    if not stable:
        mid_row = c // 2 if reverse else c // 2 - 1
        mid = cum[mid_row:mid_row + 1]
        x = cum - mid
        qe = qf * jnp.exp(x)
        ke = kf * jnp.exp(-x)
        a = _dot_nt(qe.astype(BF16), ke.astype(BF16))
        a = jnp.where((col >= row) if reverse else (row >= col), a, 0.0)
        qh = (qe * jnp.exp(mid)).astype(BF16)
        kh = (ke * jnp.exp(last - mid)).astype(BF16)
    else:
        a = jnp.where(row == col, _dot_nt(q, k), 0.0)
        c1 = cum.astype(BF16)
        r1 = cum - c1.astype(F32)
        c2 = r1.astype(BF16)
        c3 = (r1 - c2.astype(F32)).astype(BF16)
        level = 0
        while (1 << level) < c:
            m = 1 << level
            seg_row = row >> (level + 1)
            ref_row = (seg_row << (level + 1)) + (m if reverse else m - 1)
            sel = jnp.where(col == ref_row, 1.0, 0.0).astype(BF16)
            cref = _dot(sel, c1) + _dot(sel, c2) + _dot(sel, c3)
            e = jnp.exp(-jnp.abs(cum - cref))
            contrib = _dot_nt((qf * e).astype(BF16), (kf * e).astype(BF16))
            row_bit = (row >> level) & 1
            col_bit = (col >> level) & 1
            same = seg_row == (col >> (level + 1))
            if reverse:
                pick = jnp.where(same, (1 - row_bit) * col_bit, 0)
            else:
                pick = jnp.where(same, row_bit * (1 - col_bit), 0)
            a = a + jnp.where(pick == 1, contrib, 0.0)
            level += 1
        qh = (qf * jnp.exp(cum)).astype(BF16)
        kh = (kf * jnp.exp(last - cum)).astype(BF16)
    o = _dot(a.astype(BF16), v) + _dot_nt(qh, st.astype(BF16))
    st_new = st * jnp.exp(last) + _dot_tn(v, kh)
    return o, st_new


def _gla_scan_kernel(qf_ref, kf_ref, vf_ref, bf_ref, qb_ref, kb_ref, vb_ref, cb_ref, s0f_ref, s0b_ref,
                     of_ref, ob_ref, sf_ref, sb_ref):
    j = pl.program_id(2)

    @pl.when(j == 0)
    def _():
        sf_ref[...] = s0f_ref[...]
        sb_ref[...] = s0b_ref[...]

    t = qf_ref.shape[1]
    n_chunks = t // GLA_CHUNK
    n_heads = qf_ref.shape[2] // HEAD_DIM
    half = GLA_CHUNK // 2

    worst = jnp.zeros((1, qf_ref.shape[2]), F32)
    for c in range(n_chunks):
        lo = c * GLA_CHUNK
        bm = bf_ref[0, lo + half - 1:lo + half, :]
        bl = bf_ref[0, lo + GLA_CHUNK - 1:lo + GLA_CHUNK, :]
        cm = cb_ref[0, lo + half:lo + half + 1, :]
        c0 = cb_ref[0, lo:lo + 1, :]
        worst = jnp.maximum(jnp.maximum(worst, jnp.maximum(-bm, bm - bl)), jnp.maximum(-cm, cm - c0))
    steep = jnp.max(worst) > GLA_DECAY_LIMIT

    directions = ((qf_ref, kf_ref, vf_ref, bf_ref, of_ref, sf_ref, False),
                  (qb_ref, kb_ref, vb_ref, cb_ref, ob_ref, sb_ref, True))

    def chunk_step(refs, h, c, st, stable):
        q_ref, k_ref, v_ref, cum_ref, o_ref, _, reverse = refs
        ls = slice(h * HEAD_DIM, (h + 1) * HEAD_DIM)
        start = c * GLA_CHUNK
        rs = pl.ds(start if isinstance(c, int) else pl.multiple_of(start, GLA_CHUNK), GLA_CHUNK)
        o, st = _gla_chunk(q_ref[0, rs, ls], k_ref[0, rs, ls], v_ref[0, rs, ls], cum_ref[0, rs, ls],
                           st, reverse=reverse, stable=stable)
        o_ref[0, rs, ls] = o
        return st

    @pl.when(steep)
    def _():
        for refs in directions:
            for h in range(n_heads):
                def body(i, st, refs=refs, h=h):
                    c = n_chunks - 1 - i if refs[6] else i
                    return chunk_step(refs, h, c, st, True)

                refs[5][0, h] = lax.fori_loop(0, n_chunks, body, refs[5][0, h])

    @pl.when(jnp.logical_not(steep))
    def _():
        for h in range(n_heads):
            for refs in directions:
                st = refs[5][0, h]
                for i in range(n_chunks):
                    st = chunk_step(refs, h, n_chunks - 1 - i if refs[6] else i, st, False)
                refs[5][0, h] = st


def _gla_scan(q, kf, kb, v, bf, cb, s0f, s0b, t, heads_per_step):
    b, l, d = q.shape
    lanes = heads_per_step * HEAD_DIM
    nblk = l // t
    fwd = lambda bi, hp, j: (bi, j, hp)
    bwd = lambda bi, hp, j: (bi, nblk - 1 - j, hp)
    st_map = lambda bi, hp, j: (bi, hp, 0, 0)
    tok_f = pl.BlockSpec((1, t, lanes), fwd)
    tok_b = pl.BlockSpec((1, t, lanes), bwd)
    st_spec = pl.BlockSpec((1, heads_per_step, HEAD_DIM, HEAD_DIM), st_map)
    o_sd = jax.ShapeDtypeStruct((b, l, d), F32)
    s_sd = jax.ShapeDtypeStruct(s0f.shape, F32)
    return pl.pallas_call(
        _gla_scan_kernel,
        grid=(b, d // lanes, nblk),
        in_specs=[tok_f, tok_f, tok_f, tok_f, tok_b, tok_b, tok_b, tok_b, st_spec, st_spec],
        out_specs=[tok_f, tok_b, st_spec, st_spec],
        out_shape=[o_sd, o_sd, s_sd, s_sd],
        compiler_params=_params(("arbitrary", "arbitrary", "arbitrary"), 32 << 20),
        name="gla_scan",
    )(q, kf, v, bf, q, kb, v, cb, s0f, s0b)


def _ffn_route_tail(x, y, mod_ref, gffn_ref, wr_ref, br_ref, x1_ref, xs_ref, route_ref, gate_ref, cnt_ref,
                    stage, dest_vm, dest_sm, cnt_vm, cnt_sm, zero_buf, sem, *, expert_stride, fill_rows):
    tm, d = x.shape
    slabs = d // V7X_LANES
    i = pl.program_id(0)
    par = lax.rem(i, 2)
    x1 = x + mod_ref[0, 2:3, :] * y
    x1_ref[...] = x1
    hf = _rmsnorm_mod(x1, gffn_ref[...], mod_ref[0, 4:5, :], mod_ref[0, 3:4, :])
    _store_row_slabs(stage, hf, par)

    logits = _dot_nt(wr_ref[...], hf.astype(BF16)) + br_ref[:, 0:1]
    n_exp = logits.shape[0]
    eio = lax.broadcasted_iota(I32, (n_exp, tm), 0)
    vals, idxs, hots = [], [], []
    rest = logits
    for _ in range(TOP_K):
        m = jnp.max(rest, axis=0, keepdims=True)
        idx = jnp.min(jnp.where(rest == m, eio, n_exp), axis=0, keepdims=True)
        hot = eio == idx
        rest = jnp.where(hot, -jnp.inf, rest)
        vals.append(m)
        idxs.append(idx)
        hots.append(hot)
    ex = [jnp.exp(vk - vals[0]) for vk in vals]
    den = ex[0] + ex[1] + ex[2] + ex[3]
    gates = [e / den for e in ex]

    picked = jnp.zeros((n_exp, tm), F32)
    for hot in hots:
        picked = picked + jnp.where(hot, 1.0, 0.0)
    earlier = lax.broadcasted_iota(I32, (tm, tm), 0) < lax.broadcasted_iota(I32, (tm, tm), 1)
    before = _dot(picked.astype(BF16), jnp.where(earlier, 1.0, 0.0).astype(BF16)) + cnt_ref[:, 0:1]
    ranks = [jnp.sum(jnp.where(hot, before, 0.0), axis=0, keepdims=True).astype(I32) for hot in hots]
    cnt_ref[...] = cnt_ref[...] + jnp.sum(picked, axis=1, keepdims=True)

    route_ref[...] = jnp.concatenate(idxs + ranks, axis=0)
    gate_ref[...] = jnp.concatenate(gates + [jnp.zeros_like(g) for g in gates], axis=0)

    dests = [idx * expert_stride + rank for idx, rank in zip(idxs, ranks)]
    dest_vm[...] = jnp.concatenate(dests + [jnp.zeros_like(dd) for dd in dests], axis=0)
    to_smem = pltpu.make_async_copy(dest_vm, dest_sm, sem.at[2])
    to_smem.start()
    to_smem.wait()

    def issue(r, carry):
        for k in range(TOP_K):
            pltpu.make_async_copy(stage.at[par, _slab(r, slabs)], xs_ref.at[_slab(dest_sm[k, r], slabs)],
                                  sem.at[par]).start(priority=k % DMA_THREADS)
        return carry

    lax.fori_loop(0, tm, issue, 0, unroll=DMA_ISSUE_UNROLL)

    def wait_rows(parity):
        for k in range(TOP_K):
            pltpu.make_async_copy(stage.at[0], xs_ref.at[pl.ds(0, tm * slabs)], sem.at[parity]).wait()

    @pl.when(i > 0)
    def _():
        wait_rows(1 - par)

    @pl.when(i == pl.num_programs(0) - 1)
    def _():
        wait_rows(par)
        zero_buf[...] = jnp.zeros_like(zero_buf)
        cnt_vm[...] = cnt_ref[...].astype(I32)
        cnt_copy = pltpu.make_async_copy(cnt_vm, cnt_sm, sem.at[2])
        cnt_copy.start()
        cnt_copy.wait()

        def pad_rows(e):
            first = e * expert_stride + cnt_sm[e, 0]
            return xs_ref.at[pl.ds(pl.multiple_of(first * slabs, slabs), fill_rows * slabs)]

        for e in range(cnt_sm.shape[0]):
            pltpu.make_async_copy(zero_buf, pad_rows(e), sem.at[2]).start()
        for e in range(cnt_sm.shape[0]):
            pltpu.make_async_copy(zero_buf, pad_rows(e), sem.at[2]).wait()


def _hgrn_out_kernel(of_ref, ob_ref, g_ref, x_ref, mod_ref, gout_ref, wout_ref, gffn_ref, wr_ref, br_ref,
                     x1_ref, xs_ref, route_ref, gate_ref, cnt_ref, *scratch, **dispatch):
    @pl.when(pl.program_id(0) == 0)
    def _():
        cnt_ref[...] = jnp.zeros_like(cnt_ref)

    o = of_ref[...] + ob_ref[...]
    parts = []
    for h in range(o.shape[1] // HEAD_DIM):
        oh = o[:, h * HEAD_DIM:(h + 1) * HEAD_DIM]
        parts.append(oh * lax.rsqrt(jnp.mean(oh * oh, axis=-1, keepdims=True) + EPS))
    on = jnp.concatenate(parts, axis=1) * gout_ref[...]
    g = g_ref[...].astype(F32)
    a = on * (g * jax.nn.sigmoid(g))
    y = _dot(a.astype(BF16), wout_ref[...])
    _ffn_route_tail(x_ref[...], y, mod_ref, gffn_ref, wr_ref, br_ref,
                    x1_ref, xs_ref, route_ref, gate_ref, cnt_ref, *scratch, **dispatch)


def _conv_out_kernel(a_ref, x_ref, mod_ref, wout_ref, bout_ref, gffn_ref, wr_ref, br_ref,
                     x1_ref, xs_ref, route_ref, gate_ref, cnt_ref, *scratch, **dispatch):
    @pl.when(pl.program_id(0) == 0)
    def _():
        cnt_ref[...] = jnp.zeros_like(cnt_ref)

    y = _dot(a_ref[...], wout_ref[...]) + bout_ref[...]
    _ffn_route_tail(x_ref[...], y, mod_ref, gffn_ref, wr_ref, br_ref,
                    x1_ref, xs_ref, route_ref, gate_ref, cnt_ref, *scratch, **dispatch)


def _tail_specs(n, d, n_exp, tm, tiles_per_batch, fill_rows):
    tok = lambda i: (i, 0)
    const2 = lambda i: (0, 0)
    slabs = d // V7X_LANES
    expert_stride = n + fill_rows
    specs = dict(
        tok=pl.BlockSpec((tm, d), tok),
        mod=pl.BlockSpec((1, 6, d), lambda i: (i // tiles_per_batch, 0, 0)),
        vec=pl.BlockSpec((1, d), const2),
        mat=pl.BlockSpec((d, d), const2),
        wr=pl.BlockSpec((n_exp, d), const2),
        br=pl.BlockSpec((n_exp, V7X_LANES), const2),
    )
    out_specs = [pl.BlockSpec((tm, d), tok),
                 pl.BlockSpec(memory_space=pl.ANY),
                 pl.BlockSpec((8, tm), lambda i: (0, i)),
                 pl.BlockSpec((8, tm), lambda i: (0, i)),
                 pl.BlockSpec((n_exp, V7X_LANES), const2)]
    out_shape = [jax.ShapeDtypeStruct((n, d), F32),
                 jax.ShapeDtypeStruct((n_exp * expert_stride * slabs, V7X_LANES), F32),
                 jax.ShapeDtypeStruct((8, n), I32),
                 jax.ShapeDtypeStruct((8, n), F32),
                 jax.ShapeDtypeStruct((n_exp, V7X_LANES), F32)]
    scratch = [pltpu.VMEM((2, tm * slabs, V7X_LANES), F32),
               pltpu.VMEM((8, tm), I32), pltpu.SMEM((8, tm), I32),
               pltpu.VMEM((n_exp, V7X_LANES), I32), pltpu.SMEM((n_exp, V7X_LANES), I32),
               pltpu.VMEM((fill_rows * slabs, V7X_LANES), F32),
               pltpu.SemaphoreType.DMA((3,))]
    dispatch = dict(expert_stride=expert_stride, fill_rows=fill_rows)
    return specs, out_specs, out_shape, scratch, dispatch


def _hgrn_out(o_f, o_b, g, x, mods, g_out, w_out, g_ffn, wr_t, br, tm, fill_rows):
    n, d = x.shape
    s, out_specs, out_shape, scratch, dispatch = _tail_specs(
        n, d, wr_t.shape[0], tm, n // mods.shape[0] // tm, fill_rows)
    return pl.pallas_call(
        functools.partial(_hgrn_out_kernel, **dispatch),
        grid=(n // tm,),
        in_specs=[s["tok"], s["tok"], s["tok"], s["tok"], s["mod"], s["vec"], s["mat"], s["vec"], s["wr"], s["br"]],
        out_specs=out_specs,
        out_shape=out_shape,
        scratch_shapes=scratch,
        compiler_params=_params(("arbitrary",), 52 << 20),
        name="hgrn_out_route",
    )(o_f, o_b, g, x, mods, g_out, w_out, g_ffn, wr_t, br)


def _conv_out(a, x, mods, w_out, b_out, g_ffn, wr_t, br, tm, fill_rows):
    n, d = x.shape
    s, out_specs, out_shape, scratch, dispatch = _tail_specs(
        n, d, wr_t.shape[0], tm, n // mods.shape[0] // tm, fill_rows)
    return pl.pallas_call(
        functools.partial(_conv_out_kernel, **dispatch),
        grid=(n // tm,),
        in_specs=[s["tok"], s["tok"], s["mod"], s["mat"], s["vec"], s["vec"], s["wr"], s["br"]],
        out_specs=out_specs,
        out_shape=out_shape,
        scratch_shapes=scratch,
        compiler_params=_params(("arbitrary",), 44 << 20),
        name="conv_out_route",
    )(a, x, mods, w_out, b_out, g_ffn, wr_t, br)


def _routing_plan(route, counts, n_blocks, tmb, tm, expert_stride):
    n_exp = counts.shape[0]
    padded = (counts + tmb - 1) // tmb * tmb
    pend = jnp.cumsum(padded)
    pstart = pend - padded
    n_valid = pend[-1] // tmb
    blk = jnp.minimum(jnp.arange(n_blocks, dtype=I32), n_valid - 1)
    blk_e = jnp.minimum(jnp.sum((blk[:, None] * tmb >= pend[None, :]).astype(I32), axis=1), n_exp - 1)
    blk_in_e = blk - jnp.sum(jnp.where(blk_e[:, None] == jnp.arange(n_exp, dtype=I32)[None, :],
                                       (pstart // tmb)[None, :], 0), axis=1)
    blk_row = blk_e * (expert_stride // tmb) + blk_in_e
    eidx, rank = route[:TOP_K], route[TOP_K:2 * TOP_K]
    hit = eidx[None] == jnp.arange(n_exp, dtype=I32)[:, None, None]
    dest = rank + jnp.sum(jnp.where(hit, pstart[:, None, None], 0), axis=0)
    n = dest.shape[1]
    dest_tiles = dest.reshape(TOP_K, n // tm, tm).transpose(1, 0, 2).reshape(-1)
    return dest_tiles.astype(I32), blk_e.astype(I32), blk_row.astype(I32), n_valid.reshape(1).astype(I32)


def _expert_kernel(blk_e_ref, blk_row_ref, n_valid_ref, xs_ref, w1_ref, b1_ref, w2_ref, b2_ref, ys_ref, w1_bf, w2_bf):
    del blk_row_ref
    i = pl.program_id(0)
    live = i < n_valid_ref[0]
    new_expert = jnp.logical_or(i == 0, blk_e_ref[i] != blk_e_ref[jnp.maximum(i - 1, 0)])

    @pl.when(jnp.logical_and(live, new_expert))
    def _():
        w1_bf[...] = w1_ref[0, 0].astype(BF16)
        w2_bf[...] = w2_ref[0, 0].astype(BF16)

    @pl.when(live)
    def _():
        slabs = w1_bf.shape[0] // V7X_LANES
        h = _dot(_load_row_slabs(xs_ref, slabs).astype(BF16), w1_bf[...]) + b1_ref[0, 0]
        de = h.shape[1] // 2
        gate = jnp.minimum(h[:, :de], SWIGLU_LIMIT)
        up = jnp.clip(h[:, de:], -SWIGLU_LIMIT, SWIGLU_LIMIT)
        act = (up + 1.0) * gate * jax.nn.sigmoid(SWIGLU_ALPHA * gate)
        _store_row_slabs(ys_ref, _dot(act.astype(BF16), w2_bf[...]) + b2_ref[0, 0])

    @pl.when(jnp.logical_not(live))
    def _():
        ys_ref[...] = jnp.zeros_like(ys_ref)


def _experts(blk_e, blk_row, n_valid, xs, layer, w1, b1, w2, b2, n_blocks, tmb):
    _, n_exp, d, de2 = w1.shape
    slabs = d // V7X_LANES
    row_map = lambda i, be, br, nv: (br[i], 0)
    w_map = lambda i, be, br, nv: (layer, be[i], 0, 0)
    return pl.pallas_call(
        _expert_kernel,
        grid_spec=pltpu.PrefetchScalarGridSpec(
            num_scalar_prefetch=3,
            grid=(n_blocks,),
            in_specs=[pl.BlockSpec((tmb * slabs, V7X_LANES), row_map),
                      pl.BlockSpec((1, 1, d, de2), w_map),
                      pl.BlockSpec((1, 1, 1, de2), w_map),
                      pl.BlockSpec((1, 1, de2 // 2, d), w_map),
                      pl.BlockSpec((1, 1, 1, d), w_map)],
            out_specs=pl.BlockSpec((tmb * slabs, V7X_LANES), lambda i, be, br, nv: (i, 0)),
            scratch_shapes=[pltpu.VMEM((d, de2), BF16), pltpu.VMEM((de2 // 2, d), BF16)]),
        out_shape=jax.ShapeDtypeStruct((n_blocks * tmb * slabs, V7X_LANES), xs.dtype),
        compiler_params=_params(("arbitrary",), 56 << 20),
        name="moe_experts",
    )(blk_e, blk_row, n_valid, xs, w1, b1.reshape(-1, n_exp, 1, de2), w2, b2.reshape(-1, n_exp, 1, d))


def _gather_start(dest_ref, ys_ref, buf, sem, slot, tm, slabs):
    def issue(r, carry):
        for k in range(TOP_K):
            pltpu.make_async_copy(ys_ref.at[_slab(dest_ref[k * tm + r], slabs)],
                                  buf.at[slot, k, _slab(r, slabs)], sem.at[slot]).start(priority=k % DMA_THREADS)
        return carry

    lax.fori_loop(0, tm, issue, 0, unroll=DMA_ISSUE_UNROLL)


def _gather_combine(dest_ref, dest_next_ref, ys_ref, gate_ref, buf, sem):
    i = pl.program_id(0)
    slot = lax.rem(i, 2)
    tm = gate_ref.shape[0]
    slabs = buf.shape[2] // tm

    @pl.when(i == 0)
    def _():
        _gather_start(dest_ref, ys_ref, buf, sem, 0, tm, slabs)

    @pl.when(i + 1 < pl.num_programs(0))
    def _():
        _gather_start(dest_next_ref, ys_ref, buf, sem, 1 - slot, tm, slabs)

    for k in range(TOP_K):
        pltpu.make_async_copy(ys_ref.at[pl.ds(0, tm * slabs)], buf.at[slot, k], sem.at[slot]).wait()
    y = None
    for k in range(TOP_K):
        rows = _load_row_slabs(buf, slabs, slot, k)
        y = gate_ref[:, k:k + 1] * rows if y is None else y + gate_ref[:, k:k + 1] * rows
    return y


def _combine_conv_in_kernel(dest_ref, dest_next_ref, ys_ref, gate_ref, x_ref, mod_ref, modn_ref, gmix_ref,
                            win_ref, bin_ref, x2_ref, u_ref, buf, sem):
    y = _gather_combine(dest_ref, dest_next_ref, ys_ref, gate_ref, buf, sem)
    x2 = x_ref[...] + mod_ref[0, 5:6, :] * y
    x2_ref[...] = x2
    h = _rmsnorm_mod(x2, gmix_ref[...], modn_ref[0, 1:2, :], modn_ref[0, 0:1, :])
    p = _dot(h.astype(BF16), win_ref[...]) + bin_ref[...]
    d = x2.shape[1]
    u_ref[...] = p[:, :d] * jax.nn.sigmoid(p[:, d:])


def _combine_final_kernel(dest_ref, dest_next_ref, ys_ref, gate_ref, x_ref, mod_ref, gfin_ref, out_ref, buf, sem):
    y = _gather_combine(dest_ref, dest_next_ref, ys_ref, gate_ref, buf, sem)
    x4 = x_ref[...] + mod_ref[0, 5:6, :] * y
    out_ref[...] = x4 * lax.rsqrt(jnp.mean(x4 * x4, axis=-1, keepdims=True) + EPS) * gfin_ref[...]


def _combine_common(n, d, tm, tiles_per_batch):
    tok = lambda i: (i, 0)
    last = n // tm - 1
    return dict(
        dest=pl.BlockSpec((TOP_K * tm,), lambda i: (i,), memory_space=pltpu.SMEM),
        dest_next=pl.BlockSpec((TOP_K * tm,), lambda i: (jnp.minimum(i + 1, last),), memory_space=pltpu.SMEM),
        ys=pl.BlockSpec(memory_space=pl.ANY),
        gate=pl.BlockSpec((tm, 8), tok),
        tok=pl.BlockSpec((tm, d), tok),
        mod=pl.BlockSpec((1, 6, d), lambda i: (i // tiles_per_batch, 0, 0)),
        vec=pl.BlockSpec((1, d), lambda i: (0, 0)),
        scratch=[pltpu.VMEM((2, TOP_K, tm * (d // V7X_LANES), V7X_LANES), F32), pltpu.SemaphoreType.DMA((2,))],
    )


def _combine_conv_in(dest_tiles, ys, gates, x, mods, mods_next, g_mix, w_in, b_in, tm):
    n, d = x.shape
    s = _combine_common(n, d, tm, n // mods.shape[0] // tm)
    return pl.pallas_call(
        _combine_conv_in_kernel,
        grid=(n // tm,),
        in_specs=[s["dest"], s["dest_next"], s["ys"], s["gate"], s["tok"], s["mod"], s["mod"], s["vec"],
                  pl.BlockSpec((d, 2 * d), lambda i: (0, 0)),
                  pl.BlockSpec((1, 2 * d), lambda i: (0, 0))],
        out_specs=[s["tok"], s["tok"]],
        out_shape=[jax.ShapeDtypeStruct((n, d), F32), jax.ShapeDtypeStruct((n, d), F32)],
        scratch_shapes=s["scratch"],
        compiler_params=_params(("arbitrary",), 48 << 20),
        name="moe_combine_conv_in",
    )(dest_tiles, dest_tiles, ys, gates, x, mods, mods_next, g_mix, w_in, b_in)


def _combine_final(dest_tiles, ys, gates, x, mods, g_final, tm):
    n, d = x.shape
    s = _combine_common(n, d, tm, n // mods.shape[0] // tm)
    return pl.pallas_call(
        _combine_final_kernel,
        grid=(n // tm,),
        in_specs=[s["dest"], s["dest_next"], s["ys"], s["gate"], s["tok"], s["mod"], s["vec"]],
        out_specs=s["tok"],
        out_shape=jax.ShapeDtypeStruct((n, d), F32),
        scratch_shapes=s["scratch"],
        compiler_params=_params(("arbitrary",), 32 << 20),
        name="moe_combine_final",
    )(dest_tiles, dest_tiles, ys, gates, x, mods, g_final)


def _moe_experts(route, counts, xs, layer, w1, b1, w2, b2, tmb, tm):
    n = route.shape[1]
    n_exp = w1.shape[1]
    expert_stride = xs.shape[0] // (n_exp * (w1.shape[2] // V7X_LANES))
    n_blocks = (n * TOP_K + n_exp * (tmb - 1) + tmb - 1) // tmb
    dest_tiles, blk_e, blk_row, n_valid = _routing_plan(route, counts, n_blocks, tmb, tm, expert_stride)
    ys = _experts(blk_e, blk_row, n_valid, xs, layer, w1, b1, w2, b2, n_blocks, tmb)
    return dest_tiles, ys


def _dwconv_kernel(cur_ref, prev_ref, next_ref, w_ref, bdw_ref, gln_ref, bln_ref, out_ref, hbuf, vbuf, zbuf):
    i = pl.program_id(1)
    rt, w, d = cur_ref.shape[1], cur_ref.shape[2], cur_ref.shape[3]
    half = d // 2
    pad = CONV_WIDTH // 2
    n_lg = half // V7X_LANES

    for r in range(rt):
        base = r * HCONV_PITCH
        for lg in range(n_lg):
            hbuf[lg, base:base + HCONV_MARGIN, :] = jnp.zeros((HCONV_MARGIN, V7X_LANES), F32)
            hbuf[lg, base + HCONV_MARGIN:base + HCONV_MARGIN + w, :] = cur_ref[0, r, :, lg * V7X_LANES:(lg + 1) * V7X_LANES]
            hbuf[lg, base + HCONV_MARGIN + w:base + HCONV_PITCH, :] = jnp.zeros(
                (HCONV_PITCH - HCONV_MARGIN - w, V7X_LANES), F32)
    vbuf[0:rt] = jnp.where(i > 0, prev_ref[0], 0.0)
    vbuf[rt:2 * rt] = cur_ref[0, :, :, half:d]
    vbuf[2 * rt:3 * rt] = jnp.where(i < pl.num_programs(1) - 1, next_ref[0], 0.0)

    def col_body(wi, carry):
        for lg in range(n_lg):
            ls = slice(lg * V7X_LANES, (lg + 1) * V7X_LANES)
            acc = jnp.zeros((rt, V7X_LANES), F32)
            for j in range(CONV_WIDTH):
                taps = hbuf[lg, pl.ds(wi + (HCONV_MARGIN - pad + j), rt, stride=HCONV_PITCH), :]
                acc = acc + taps * w_ref[j:j + 1, ls]
            zbuf[lg, pl.ds(wi, rt, stride=w), :] = acc
        return carry

    lax.fori_loop(0, w, col_body, 0)

    def row_body(r, carry):
        def piece_body(p, carry_p):
            w0 = pl.multiple_of(p * VCONV_PIECE, VCONV_PIECE)
            for lg in range(n_lg):
                ls = slice(lg * V7X_LANES, (lg + 1) * V7X_LANES)
                ls_hi = slice(half + lg * V7X_LANES, half + (lg + 1) * V7X_LANES)
                acc = jnp.zeros((VCONV_PIECE, V7X_LANES), F32)
                for j in range(CONV_WIDTH):
                    acc = acc + vbuf[rt - pad + j + r, pl.ds(w0, VCONV_PIECE), ls] * w_ref[j:j + 1, ls_hi]
                zbuf[n_lg + lg, pl.ds(pl.multiple_of(r * w + w0, VCONV_PIECE), VCONV_PIECE), :] = acc
            return carry_p

        return lax.fori_loop(0, w // VCONV_PIECE, piece_body, carry)

    lax.fori_loop(0, rt, row_body, 0)

    z = jnp.concatenate([zbuf[s] for s in range(d // V7X_LANES)], axis=1) + bdw_ref[...]
    mu = jnp.mean(z, axis=-1, keepdims=True)
    zc = z - mu
    y = zc * lax.rsqrt(jnp.mean(zc * zc, axis=-1, keepdims=True) + EPS) * gln_ref[...] + bln_ref[...]
    out_ref[0] = (y * jax.nn.sigmoid(y)).astype(BF16).reshape(rt, w, d)


def _dwconv(u, w_dw, b_dw, g_ln, b_ln, rt):
    b, r, w, d = u.shape
    half = d // 2
    nb = r // rt
    vec = pl.BlockSpec((1, d), lambda bi, i: (0, 0))
    return pl.pallas_call(
        _dwconv_kernel,
        grid=(b, nb),
        in_specs=[pl.BlockSpec((1, rt, w, d), lambda bi, i: (bi, i, 0, 0)),
                  pl.BlockSpec((1, rt, w, half), lambda bi, i: (bi, jnp.maximum(i - 1, 0), 0, 1)),
                  pl.BlockSpec((1, rt, w, half), lambda bi, i: (bi, jnp.minimum(i + 1, nb - 1), 0, 1)),
                  pl.BlockSpec((CONV_WIDTH, d), lambda bi, i: (0, 0)),
                  vec, vec, vec],
        out_specs=pl.BlockSpec((1, rt, w, d), lambda bi, i: (bi, i, 0, 0)),
        out_shape=jax.ShapeDtypeStruct((b, r, w, d), BF16),
        scratch_shapes=[pltpu.VMEM((half // V7X_LANES, rt * HCONV_PITCH, V7X_LANES), F32),
                        pltpu.VMEM((3 * rt, w, half), F32),
                        pltpu.VMEM((d // V7X_LANES, rt * w, V7X_LANES), F32)],
        compiler_params=_params(("arbitrary", "arbitrary"), 48 << 20),
        name="dwconv_ln_silu",
    )(u, u, u, w_dw, b_dw, g_ln, b_ln)


def kernel(x, c, ctx, c_ctx, w_ada, b_ada, g_mix, g_ffn, w_hgrn_in, hgrn_gamma, g_hgrn_out, w_hgrn_out,
           w_cv_in, b_cv_in, w_cv_dw, b_cv_dw, g_cv_ln, b_cv_ln, w_cv_out, b_cv_out,
           w_router, b_router, w_exp_in, b_exp_in, w_exp_out, b_exp_out, g_final):
    bsz, seq, d = x.shape
    n = bsz * seq
    n_heads = d // HEAD_DIM
    n_exp = w_router.shape[-1]
    assert bsz + 1 <= 8 and seq % 512 == 0 and ctx.shape[1] % 256 == 0

    cond = jnp.zeros((8, d), F32).at[:bsz].set(c).at[bsz].set(c_ctx)
    mods = _ada_mods(cond, w_ada, b_ada).reshape(w_ada.shape[0], 8, 6, d)
    mods_lat = [mods[l, :bsz] for l in range(w_ada.shape[0])]
    mods_ctx = jnp.broadcast_to(mods[0, bsz:bsz + 1], (bsz, 6, d))

    lb_all = jnp.cumsum(jax.nn.softmax(hgrn_gamma.astype(F32), axis=1), axis=1)
    lb = lb_all[:, 0, :]

    row = lambda v: v.reshape(1, -1)
    wr_t = [w_router[l].T.astype(BF16) for l in range(2)]
    br = [jnp.broadcast_to(b_router[l][:, None], (n_exp, V7X_LANES)) for l in range(2)]

    w_in = w_hgrn_in[0].astype(BF16)
    scan_t = 256
    zeros_state = jnp.zeros((bsz, n_heads, HEAD_DIM, HEAD_DIM), F32)
    pc = _hgrn_proj(ctx, mods_ctx, row(g_mix[0]), lb, w_in, 256)
    _, _, s_f, s_b = _gla_scan(pc[0], pc[1], pc[2], pc[3], pc[4], pc[5], zeros_state, zeros_state, scan_t, n_heads)
    pq, pkf, pkb, pv, pbf, pcb, pg = _hgrn_proj(x, mods_lat[0], row(g_mix[0]), lb, w_in, 256)
    o_f, o_b, _, _ = _gla_scan(pq, pkf, pkb, pv, pbf, pcb, s_f, s_b, scan_t, n_heads)

    tm_tail, tmb, tm_rows = 512, 512, 256
    x1, xs, route, gates, cnt = _hgrn_out(
        o_f.reshape(n, d), o_b.reshape(n, d), pg.reshape(n, d), x.reshape(n, d), mods_lat[0],
        row(g_hgrn_out[0]), w_hgrn_out[0].astype(BF16), row(g_ffn[0]), wr_t[0], br[0], tm_tail, tmb)
    dest_tiles, ys = _moe_experts(route, cnt[:, 0].astype(I32), xs, 0,
                                  w_exp_in, b_exp_in, w_exp_out, b_exp_out, tmb, tm_rows)

    x2, u = _combine_conv_in(dest_tiles, ys, gates.T, x1, mods_lat[0], mods_lat[1], row(g_mix[1]),
                             w_cv_in[0].astype(BF16), row(b_cv_in[0]), tm_rows)
    za = _dwconv(u.reshape(bsz, seq // GRID_W, GRID_W, d), w_cv_dw[0], row(b_cv_dw[0]),
                 row(g_cv_ln[0]), row(b_cv_ln[0]), 16)
    x3, xs, route, gates, cnt = _conv_out(
        za.reshape(n, d), x2, mods_lat[1], w_cv_out[0].astype(BF16), row(b_cv_out[0]),
        row(g_ffn[1]), wr_t[1], br[1], tm_tail, tmb)
    dest_tiles, ys = _moe_experts(route, cnt[:, 0].astype(I32), xs, 1,
                                  w_exp_in, b_exp_in, w_exp_out, b_exp_out, tmb, tm_rows)
    out = _combine_final(dest_tiles, ys, gates.T, x3, mods_lat[1], row(g_final), tm_rows)
    return out.reshape(bsz, seq, d)
```

```python
import functools

import jax
import jax.numpy as jnp
from jax import lax
from jax.experimental import pallas as pl
from jax.experimental.pallas import tpu as pltpu

F32 = jnp.float32
BF16 = jnp.bfloat16
I32 = jnp.int32
U32 = jnp.uint32

EPS = 1e-6
HEAD_DIM = 128
GLA_CHUNK = 128
GLA_DECAY_LIMIT = 60.0
DMA_ISSUE_UNROLL = 8
DMA_THREADS = 2
GRID_W = 64
CONV_WIDTH = 31
VCONV_PIECE = 16
HCONV_MARGIN = 16
HCONV_PITCH = 100
TOP_K = 4
SWIGLU_ALPHA = 1.702
SWIGLU_LIMIT = 7.0

V7X_LANES = 128
V7X_VMEM_BYTES = 64 * 1024 * 1024
VMEM_HEADROOM_BYTES = 8 * 1024 * 1024


def _params(semantics, vmem_bytes):
    limit = min(int(vmem_bytes), V7X_VMEM_BYTES - VMEM_HEADROOM_BYTES)
    return pltpu.CompilerParams(dimension_semantics=semantics, vmem_limit_bytes=limit)


def _dot(a, b):
    return jnp.dot(a, b, preferred_element_type=F32)


def _dot_nt(a, b):
    return lax.dot_general(a, b, (((1,), (1,)), ((), ())), preferred_element_type=F32)


def _dot_tn(a, b):
    return lax.dot_general(a, b, (((0,), (0,)), ((), ())), preferred_element_type=F32)


def _rmsnorm_mod(x, g, scale, shift):
    y = x * lax.rsqrt(jnp.mean(x * x, axis=-1, keepdims=True) + EPS)
    return y * g * (1.0 + scale) + shift


def _store_row_slabs(ref, x, *lead):
    m, d = x.shape
    slabs = d // V7X_LANES
    for s in range(slabs):
        ref[(*lead, pl.ds(s, m, stride=slabs), slice(None))] = x[:, s * V7X_LANES:(s + 1) * V7X_LANES]


def _load_row_slabs(ref, slabs, *lead):
    m = ref.shape[-2] // slabs
    return jnp.concatenate([ref[(*lead, pl.ds(s, m, stride=slabs), slice(None))] for s in range(slabs)], axis=1)


def _slab(row, slabs):
    return pl.ds(pl.multiple_of(row * slabs, slabs), slabs)


def _ada_kernel(c_ref, w_ref, b_ref, o_ref):
    s = c_ref[...]
    s = s * jax.nn.sigmoid(s)
    o_ref[0] = jnp.dot(s, w_ref[0], preferred_element_type=F32,
                       precision=lax.Precision.HIGHEST) + b_ref[0]


def _ada_mods(cond, w_ada, b_ada):
    depth, d, d6 = w_ada.shape
    rows = cond.shape[0]
    tn = 1536
    return pl.pallas_call(
        _ada_kernel,
        grid=(depth, d6 // tn),
        in_specs=[pl.BlockSpec((rows, d), lambda l, n: (0, 0)),
                  pl.BlockSpec((1, d, tn), lambda l, n: (l, 0, n)),
                  pl.BlockSpec((1, 1, tn), lambda l, n: (l, 0, n))],
        out_specs=pl.BlockSpec((1, rows, tn), lambda l, n: (l, 0, n)),
        out_shape=jax.ShapeDtypeStruct((depth, rows, d6), F32),
        compiler_params=_params(("arbitrary", "arbitrary"), 24 << 20),
        name="ada_mods",
    )(cond, w_ada, b_ada.reshape(depth, 1, d6))


def _chunk_cumsum(x, chunk, reverse):
    rows, cols = x.shape
    pos = lax.broadcasted_iota(I32, (rows, V7X_LANES), 0) & (chunk - 1)
    step = 1
    while step < chunk:
        keep = (pos < chunk - step) if reverse else (pos >= step)
        shift = rows - step if reverse else step
        parts = []
        for c in range(cols // V7X_LANES):
            xs = x[:, c * V7X_LANES:(c + 1) * V7X_LANES]
            parts.append(xs + jnp.where(keep, pltpu.roll(xs, shift, 0), 0.0))
        x = jnp.concatenate(parts, axis=1)
        step *= 2
    return x


def _hgrn_proj_kernel(x_ref, mod_ref, gmix_ref, lb_ref, w_ref,
                      q_ref, kf_ref, kb_ref, v_ref, bf_ref, cb_ref, g_ref):
    d = x_ref.shape[-1]
    h = _rmsnorm_mod(x_ref[0], gmix_ref[...], mod_ref[0, 1:2, :], mod_ref[0, 0:1, :])
    p = _dot(h.astype(BF16), w_ref[...])
    q_ref[0] = p[:, 0:d].astype(BF16)
    v_ref[0] = p[:, d:2 * d].astype(BF16)
    g_ref[0] = p[:, 4 * d:5 * d].astype(BF16)
    lbf = lb_ref[0:1, :]
    lbb = lb_ref[1:2, :]
    ff = lbf + (1.0 - lbf) * jax.nn.sigmoid(p[:, 2 * d:3 * d])
    fb = lbb + (1.0 - lbb) * jax.nn.sigmoid(p[:, 3 * d:4 * d])
    kf_ref[0] = (1.0 - ff).astype(BF16)
    kb_ref[0] = (1.0 - fb).astype(BF16)
    bf_ref[0] = _chunk_cumsum(jnp.log(ff), GLA_CHUNK, reverse=False)
    cb_ref[0] = _chunk_cumsum(jnp.log(fb), GLA_CHUNK, reverse=True)


def _hgrn_proj(x, mods, g_mix, lb, w_in, tm):
    b, l, d = x.shape
    d5 = w_in.shape[1]
    tok = lambda bi, i: (bi, i, 0)
    const2 = lambda bi, i: (0, 0)
    bf_sd = jax.ShapeDtypeStruct((b, l, d), BF16)
    f_sd = jax.ShapeDtypeStruct((b, l, d), F32)
    return pl.pallas_call(
        _hgrn_proj_kernel,
        grid=(b, l // tm),
        in_specs=[pl.BlockSpec((1, tm, d), tok),
                  pl.BlockSpec((1, 6, d), lambda bi, i: (bi, 0, 0)),
                  pl.BlockSpec((1, d), const2),
                  pl.BlockSpec((2, d), const2),
                  pl.BlockSpec((d, d5), const2)],
        out_specs=[pl.BlockSpec((1, tm, d), tok)] * 7,
        out_shape=[bf_sd, bf_sd, bf_sd, bf_sd, f_sd, f_sd, bf_sd],
        compiler_params=_params(("arbitrary", "arbitrary"), 52 << 20),
        name="hgrn_proj",
    )(x, mods, g_mix, lb, w_in)


def _gla_chunk(q, k, v, cum, st, *, reverse, stable):
    c = q.shape[0]
    qf = q.astype(F32)
    kf = k.astype(F32)
    last = cum[0:1] if reverse else cum[c - 1:c]
    row = lax.broadcasted_iota(I32, (c, c), 0)
    col = lax.broadcasted_iota(I32, (c, c), 1)
    if not stable:
        mid_row = c // 2 if reverse else c // 2 - 1
        mid = cum[mid_row:mid_row + 1]
        x = cum - mid
        qe = qf * jnp.exp(x)
        ke = kf * jnp.exp(-x)
        a = _dot_nt(qe.astype(BF16), ke.astype(BF16))
        a = jnp.where((col >= row) if reverse else (row >= col), a, 0.0)
        qh = (qe * jnp.exp(mid)).astype(BF16)
        kh = (ke * jnp.exp(last - mid)).astype(BF16)
    else:
        a = jnp.where(row == col, _dot_nt(q, k), 0.0)
        c1 = cum.astype(BF16)
        r1 = cum - c1.astype(F32)
        c2 = r1.astype(BF16)
        c3 = (r1 - c2.astype(F32)).astype(BF16)
        level = 0
        while (1 << level) < c:
            m = 1 << level
            seg_row = row >> (level + 1)
            ref_row = (seg_row << (level + 1)) + (m if reverse else m - 1)
            sel = jnp.where(col == ref_row, 1.0, 0.0).astype(BF16)
            cref = _dot(sel, c1) + _dot(sel, c2) + _dot(sel, c3)
            e = jnp.exp(-jnp.abs(cum - cref))
            contrib = _dot_nt((qf * e).astype(BF16), (kf * e).astype(BF16))
            row_bit = (row >> level) & 1
            col_bit = (col >> level) & 1
            same = seg_row == (col >> (level + 1))
            if reverse:
                pick = jnp.where(same, (1 - row_bit) * col_bit, 0)
            else:
                pick = jnp.where(same, row_bit * (1 - col_bit), 0)
            a = a + jnp.where(pick == 1, contrib, 0.0)
            level += 1
        qh = (qf * jnp.exp(cum)).astype(BF16)
        kh = (kf * jnp.exp(last - cum)).astype(BF16)
    o = _dot(a.astype(BF16), v) + _dot_nt(qh, st.astype(BF16))
    st_new = st * jnp.exp(last) + _dot_tn(v, kh)
    return o, st_new


def _gla_scan_kernel(qf_ref, kf_ref, vf_ref, bf_ref, qb_ref, kb_ref, vb_ref, cb_ref, s0f_ref, s0b_ref,
                     of_ref, ob_ref, sf_ref, sb_ref):
    j = pl.program_id(2)

    @pl.when(j == 0)
    def _():
        sf_ref[...] = s0f_ref[...]
        sb_ref[...] = s0b_ref[...]

    t = qf_ref.shape[1]
    n_chunks = t // GLA_CHUNK
    n_heads = qf_ref.shape[2] // HEAD_DIM
    half = GLA_CHUNK // 2

    worst = jnp.zeros((1, qf_ref.shape[2]), F32)
    for c in range(n_chunks):
        lo = c * GLA_CHUNK
        bm = bf_ref[0, lo + half - 1:lo + half, :]
        bl = bf_ref[0, lo + GLA_CHUNK - 1:lo + GLA_CHUNK, :]
        cm = cb_ref[0, lo + half:lo + half + 1, :]
        c0 = cb_ref[0, lo:lo + 1, :]
        worst = jnp.maximum(jnp.maximum(worst, jnp.maximum(-bm, bm - bl)), jnp.maximum(-cm, cm - c0))
    steep = jnp.max(worst) > GLA_DECAY_LIMIT

    directions = ((qf_ref, kf_ref, vf_ref, bf_ref, of_ref, sf_ref, False),
                  (qb_ref, kb_ref, vb_ref, cb_ref, ob_ref, sb_ref, True))

    def chunk_step(refs, h, c, st, stable):
        q_ref, k_ref, v_ref, cum_ref, o_ref, _, reverse = refs
        ls = slice(h * HEAD_DIM, (h + 1) * HEAD_DIM)
        start = c * GLA_CHUNK
        rs = pl.ds(start if isinstance(c, int) else pl.multiple_of(start, GLA_CHUNK), GLA_CHUNK)
        o, st = _gla_chunk(q_ref[0, rs, ls], k_ref[0, rs, ls], v_ref[0, rs, ls], cum_ref[0, rs, ls],
                           st, reverse=reverse, stable=stable)
        o_ref[0, rs, ls] = o
        return st

    @pl.when(steep)
    def _():
        for refs in directions:
            for h in range(n_heads):
                def body(i, st, refs=refs, h=h):
                    c = n_chunks - 1 - i if refs[6] else i
                    return chunk_step(refs, h, c, st, True)

                refs[5][0, h] = lax.fori_loop(0, n_chunks, body, refs[5][0, h])

    @pl.when(jnp.logical_not(steep))
    def _():
        for h in range(n_heads):
            for refs in directions:
                st = refs[5][0, h]
                for i in range(n_chunks):
                    st = chunk_step(refs, h, n_chunks - 1 - i if refs[6] else i, st, False)
                refs[5][0, h] = st


def _gla_scan(q, kf, kb, v, bf, cb, s0f, s0b, t, heads_per_step):
    b, l, d = q.shape
    lanes = heads_per_step * HEAD_DIM
    nblk = l // t
    fwd = lambda bi, hp, j: (bi, j, hp)
    bwd = lambda bi, hp, j: (bi, nblk - 1 - j, hp)
    st_map = lambda bi, hp, j: (bi, hp, 0, 0)
    tok_f = pl.BlockSpec((1, t, lanes), fwd)
    tok_b = pl.BlockSpec((1, t, lanes), bwd)
    st_spec = pl.BlockSpec((1, heads_per_step, HEAD_DIM, HEAD_DIM), st_map)
    o_sd = jax.ShapeDtypeStruct((b, l, d), F32)
    s_sd = jax.ShapeDtypeStruct(s0f.shape, F32)
    return pl.pallas_call(
        _gla_scan_kernel,
        grid=(b, d // lanes, nblk),
        in_specs=[tok_f, tok_f, tok_f, tok_f, tok_b, tok_b, tok_b, tok_b, st_spec, st_spec],
        out_specs=[tok_f, tok_b, st_spec, st_spec],
        out_shape=[o_sd, o_sd, s_sd, s_sd],
        compiler_params=_params(("arbitrary", "arbitrary", "arbitrary"), 32 << 20),
        name="gla_scan",
    )(q, kf, v, bf, q, kb, v, cb, s0f, s0b)


def _ffn_route_tail(x, y, mod_ref, gffn_ref, wr_ref, br_ref, x1_ref, xs_ref, route_ref, gate_ref, cnt_ref,
                    stage, dest_vm, dest_sm, cnt_vm, cnt_sm, zero_buf, sem, *, expert_stride, fill_rows):
    tm, d = x.shape
    slabs = d // V7X_LANES
    i = pl.program_id(0)
    par = lax.rem(i, 2)
    x1 = x + mod_ref[0, 2:3, :] * y
    x1_ref[...] = x1
    hf = _rmsnorm_mod(x1, gffn_ref[...], mod_ref[0, 4:5, :], mod_ref[0, 3:4, :])
    _store_row_slabs(stage, hf, par)

    logits = _dot_nt(wr_ref[...], hf.astype(BF16)) + br_ref[:, 0:1]
    n_exp = logits.shape[0]
    eio = lax.broadcasted_iota(I32, (n_exp, tm), 0)
    vals, idxs, hots = [], [], []
    rest = logits
    for _ in range(TOP_K):
        m = jnp.max(rest, axis=0, keepdims=True)
        idx = jnp.min(jnp.where(rest == m, eio, n_exp), axis=0, keepdims=True)
        hot = eio == idx
        rest = jnp.where(hot, -jnp.inf, rest)
        vals.append(m)
        idxs.append(idx)
        hots.append(hot)
    ex = [jnp.exp(vk - vals[0]) for vk in vals]
    den = ex[0] + ex[1] + ex[2] + ex[3]
    gates = [e / den for e in ex]

    picked = jnp.zeros((n_exp, tm), F32)
    for hot in hots:
        picked = picked + jnp.where(hot, 1.0, 0.0)
    earlier = lax.broadcasted_iota(I32, (tm, tm), 0) < lax.broadcasted_iota(I32, (tm, tm), 1)
    before = _dot(picked.astype(BF16), jnp.where(earlier, 1.0, 0.0).astype(BF16)) + cnt_ref[:, 0:1]
    ranks = [jnp.sum(jnp.where(hot, before, 0.0), axis=0, keepdims=True).astype(I32) for hot in hots]
    cnt_ref[...] = cnt_ref[...] + jnp.sum(picked, axis=1, keepdims=True)

    route_ref[...] = jnp.concatenate(idxs + ranks, axis=0)
    gate_ref[...] = jnp.concatenate(gates + [jnp.zeros_like(g) for g in gates], axis=0)

    dests = [(idx * expert_stride + rank) * slabs for idx, rank in zip(idxs, ranks)]
    dest_vm[...] = jnp.concatenate(dests + [jnp.zeros_like(dd) for dd in dests], axis=0)
    to_smem = pltpu.make_async_copy(dest_vm, dest_sm, sem.at[2])
    to_smem.start()
    to_smem.wait()

    def issue(r, carry):
        for k in range(TOP_K):
            pltpu.make_async_copy(stage.at[par, _slab(r, slabs)],
                                  xs_ref.at[pl.ds(pl.multiple_of(dest_sm[k, r], slabs), slabs)],
                                  sem.at[par]).start(priority=k % DMA_THREADS)
        return carry

    lax.fori_loop(0, tm, issue, 0, unroll=DMA_ISSUE_UNROLL)

    def wait_rows(parity):
        for k in range(TOP_K):
            pltpu.make_async_copy(stage.at[0], xs_ref.at[pl.ds(0, tm * slabs)], sem.at[parity]).wait()

    @pl.when(i > 0)
    def _():
        wait_rows(1 - par)

    @pl.when(i == pl.num_programs(0) - 1)
    def _():
        wait_rows(par)
        zero_buf[...] = jnp.zeros_like(zero_buf)
        cnt_vm[...] = cnt_ref[...].astype(I32)
        cnt_copy = pltpu.make_async_copy(cnt_vm, cnt_sm, sem.at[2])
        cnt_copy.start()
        cnt_copy.wait()

        def pad_rows(e):
            first = e * expert_stride + cnt_sm[e, 0]
            return xs_ref.at[pl.ds(pl.multiple_of(first * slabs, slabs), fill_rows * slabs)]

        for e in range(cnt_sm.shape[0]):
            pltpu.make_async_copy(zero_buf, pad_rows(e), sem.at[2]).start()
        for e in range(cnt_sm.shape[0]):
            pltpu.make_async_copy(zero_buf, pad_rows(e), sem.at[2]).wait()


def _hgrn_out_kernel(of_ref, ob_ref, g_ref, x_ref, mod_ref, gout_ref, wout_ref, gffn_ref, wr_ref, br_ref,
                     x1_ref, xs_ref, route_ref, gate_ref, cnt_ref, *scratch, **dispatch):
    @pl.when(pl.program_id(0) == 0)
    def _():
        cnt_ref[...] = jnp.zeros_like(cnt_ref)

    o = of_ref[...] + ob_ref[...]
    parts = []
    for h in range(o.shape[1] // HEAD_DIM):
        oh = o[:, h * HEAD_DIM:(h + 1) * HEAD_DIM]
        parts.append(oh * lax.rsqrt(jnp.mean(oh * oh, axis=-1, keepdims=True) + EPS))
    on = jnp.concatenate(parts, axis=1) * gout_ref[...]
    g = g_ref[...].astype(F32)
    a = on * (g * jax.nn.sigmoid(g))
    y = _dot(a.astype(BF16), wout_ref[...])
    _ffn_route_tail(x_ref[...], y, mod_ref, gffn_ref, wr_ref, br_ref,
                    x1_ref, xs_ref, route_ref, gate_ref, cnt_ref, *scratch, **dispatch)


def _conv_out_kernel(a_ref, x_ref, mod_ref, wout_ref, bout_ref, gffn_ref, wr_ref, br_ref,
                     x1_ref, xs_ref, route_ref, gate_ref, cnt_ref, *scratch, **dispatch):
    @pl.when(pl.program_id(0) == 0)
    def _():
        cnt_ref[...] = jnp.zeros_like(cnt_ref)

    y = _dot(a_ref[...], wout_ref[...]) + bout_ref[...]
    _ffn_route_tail(x_ref[...], y, mod_ref, gffn_ref, wr_ref, br_ref,
                    x1_ref, xs_ref, route_ref, gate_ref, cnt_ref, *scratch, **dispatch)


def _tail_specs(n, d, n_exp, tm, tiles_per_batch, fill_rows):
    tok = lambda i: (i, 0)
    const2 = lambda i: (0, 0)
    slabs = d // V7X_LANES
    expert_stride = n + fill_rows
    specs = dict(
        tok=pl.BlockSpec((tm, d), tok),
        mod=pl.BlockSpec((1, 6, d), lambda i: (i // tiles_per_batch, 0, 0)),
        vec=pl.BlockSpec((1, d), const2),
        mat=pl.BlockSpec((d, d), const2),
        wr=pl.BlockSpec((n_exp, d), const2),
        br=pl.BlockSpec((n_exp, V7X_LANES), const2),
    )
    out_specs = [pl.BlockSpec((tm, d), tok),
                 pl.BlockSpec(memory_space=pl.ANY),
                 pl.BlockSpec((8, tm), lambda i: (0, i)),
                 pl.BlockSpec((8, tm), lambda i: (0, i)),
                 pl.BlockSpec((n_exp, V7X_LANES), const2)]
    out_shape = [jax.ShapeDtypeStruct((n, d), F32),
                 jax.ShapeDtypeStruct((n_exp * expert_stride * slabs, V7X_LANES), F32),
                 jax.ShapeDtypeStruct((8, n), I32),
                 jax.ShapeDtypeStruct((8, n), F32),
                 jax.ShapeDtypeStruct((n_exp, V7X_LANES), F32)]
    scratch = [pltpu.VMEM((2, tm * slabs, V7X_LANES), F32),
               pltpu.VMEM((8, tm), I32), pltpu.SMEM((8, tm), I32),
               pltpu.VMEM((n_exp, V7X_LANES), I32), pltpu.SMEM((n_exp, V7X_LANES), I32),
               pltpu.VMEM((fill_rows * slabs, V7X_LANES), F32),
               pltpu.SemaphoreType.DMA((3,))]
    dispatch = dict(expert_stride=expert_stride, fill_rows=fill_rows)
    return specs, out_specs, out_shape, scratch, dispatch


def _hgrn_out(o_f, o_b, g, x, mods, g_out, w_out, g_ffn, wr_t, br, tm, fill_rows):
    n, d = x.shape
    s, out_specs, out_shape, scratch, dispatch = _tail_specs(
        n, d, wr_t.shape[0], tm, n // mods.shape[0] // tm, fill_rows)
    return pl.pallas_call(
        functools.partial(_hgrn_out_kernel, **dispatch),
        grid=(n // tm,),
        in_specs=[s["tok"], s["tok"], s["tok"], s["tok"], s["mod"], s["vec"], s["mat"], s["vec"], s["wr"], s["br"]],
        out_specs=out_specs,
        out_shape=out_shape,
        scratch_shapes=scratch,
        compiler_params=_params(("arbitrary",), 52 << 20),
        name="hgrn_out_route",
    )(o_f, o_b, g, x, mods, g_out, w_out, g_ffn, wr_t, br)


def _conv_out(a, x, mods, w_out, b_out, g_ffn, wr_t, br, tm, fill_rows):
    n, d = x.shape
    s, out_specs, out_shape, scratch, dispatch = _tail_specs(
        n, d, wr_t.shape[0], tm, n // mods.shape[0] // tm, fill_rows)
    return pl.pallas_call(
        functools.partial(_conv_out_kernel, **dispatch),
        grid=(n // tm,),
        in_specs=[s["tok"], s["tok"], s["mod"], s["mat"], s["vec"], s["vec"], s["wr"], s["br"]],
        out_specs=out_specs,
        out_shape=out_shape,
        scratch_shapes=scratch,
        compiler_params=_params(("arbitrary",), 44 << 20),
        name="conv_out_route",
    )(a, x, mods, w_out, b_out, g_ffn, wr_t, br)


def _routing_plan(route, counts, n_blocks, tmb, tm, expert_stride, slabs):
    n_exp = counts.shape[0]
    padded = (counts + tmb - 1) // tmb * tmb
    pend = jnp.cumsum(padded)
    pstart = pend - padded
    n_valid = pend[-1] // tmb
    blk = jnp.minimum(jnp.arange(n_blocks, dtype=I32), n_valid - 1)
    blk_e = jnp.minimum(jnp.sum((blk[:, None] * tmb >= pend[None, :]).astype(I32), axis=1), n_exp - 1)
    blk_in_e = blk - jnp.sum(jnp.where(blk_e[:, None] == jnp.arange(n_exp, dtype=I32)[None, :],
                                       (pstart // tmb)[None, :], 0), axis=1)
    blk_row = blk_e * (expert_stride // tmb) + blk_in_e
    eidx, rank = route[:TOP_K], route[TOP_K:2 * TOP_K]
    hit = eidx[None] == jnp.arange(n_exp, dtype=I32)[:, None, None]
    dest = (rank + jnp.sum(jnp.where(hit, pstart[:, None, None], 0), axis=0)) * slabs
    n = dest.shape[1]
    dest_tiles = dest.reshape(TOP_K, n // tm, tm).transpose(1, 0, 2).reshape(-1)
    return dest_tiles.astype(I32), blk_e.astype(I32), blk_row.astype(I32), n_valid.reshape(1).astype(I32)


def _expert_kernel(blk_e_ref, blk_row_ref, n_valid_ref, xs_ref, w1_ref, b1_ref, w2_ref, b2_ref, ys_ref, w1_bf, w2_bf):
    del blk_row_ref
    i = pl.program_id(0)
    live = i < n_valid_ref[0]
    new_expert = jnp.logical_or(i == 0, blk_e_ref[i] != blk_e_ref[jnp.maximum(i - 1, 0)])

    @pl.when(jnp.logical_and(live, new_expert))
    def _():
        w1_bf[...] = w1_ref[0, 0].astype(BF16)
        w2_bf[...] = w2_ref[0, 0].astype(BF16)

    @pl.when(live)
    def _():
        slabs = w1_bf.shape[0] // V7X_LANES
        h = _dot(_load_row_slabs(xs_ref, slabs).astype(BF16), w1_bf[...]) + b1_ref[0, 0]
        de = h.shape[1] // 2
        gate = jnp.minimum(h[:, :de], SWIGLU_LIMIT)
        up = jnp.clip(h[:, de:], -SWIGLU_LIMIT, SWIGLU_LIMIT)
        act = (up + 1.0) * gate * jax.nn.sigmoid(SWIGLU_ALPHA * gate)
        _store_row_slabs(ys_ref, _dot(act.astype(BF16), w2_bf[...]) + b2_ref[0, 0])

    @pl.when(jnp.logical_not(live))
    def _():
        ys_ref[...] = jnp.zeros_like(ys_ref)


def _experts(blk_e, blk_row, n_valid, xs, layer, w1, b1, w2, b2, n_blocks, tmb):
    _, n_exp, d, de2 = w1.shape
    slabs = d // V7X_LANES
    row_map = lambda i, be, br, nv: (br[i], 0)
    w_map = lambda i, be, br, nv: (layer, be[i], 0, 0)
    return pl.pallas_call(
        _expert_kernel,
        grid_spec=pltpu.PrefetchScalarGridSpec(
            num_scalar_prefetch=3,
            grid=(n_blocks,),
            in_specs=[pl.BlockSpec((tmb * slabs, V7X_LANES), row_map),
                      pl.BlockSpec((1, 1, d, de2), w_map),
                      pl.BlockSpec((1, 1, 1, de2), w_map),
                      pl.BlockSpec((1, 1, de2 // 2, d), w_map),
                      pl.BlockSpec((1, 1, 1, d), w_map)],
            out_specs=pl.BlockSpec((tmb * slabs, V7X_LANES), lambda i, be, br, nv: (i, 0)),
            scratch_shapes=[pltpu.VMEM((d, de2), BF16), pltpu.VMEM((de2 // 2, d), BF16)]),
        out_shape=jax.ShapeDtypeStruct((n_blocks * tmb * slabs, V7X_LANES), xs.dtype),
        compiler_params=_params(("arbitrary",), 56 << 20),
        name="moe_experts",
    )(blk_e, blk_row, n_valid, xs, w1, b1.reshape(-1, n_exp, 1, de2), w2, b2.reshape(-1, n_exp, 1, d))


def _gather_start(dest_ref, ys_ref, buf, sem, slot, tm, slabs):
    def issue(r, carry):
        for k in range(TOP_K):
            pltpu.make_async_copy(ys_ref.at[pl.ds(pl.multiple_of(dest_ref[k * tm + r], slabs), slabs)],
                                  buf.at[slot, k, _slab(r, slabs)], sem.at[slot]).start(priority=k % DMA_THREADS)
        return carry

    lax.fori_loop(0, tm, issue, 0, unroll=DMA_ISSUE_UNROLL)


def _gather_combine(dest_ref, dest_next_ref, ys_ref, gate_ref, buf, sem):
    i = pl.program_id(0)
    slot = lax.rem(i, 2)
    tm = gate_ref.shape[0]
    slabs = buf.shape[2] // tm

    @pl.when(i == 0)
    def _():
        _gather_start(dest_ref, ys_ref, buf, sem, 0, tm, slabs)

    @pl.when(i + 1 < pl.num_programs(0))
    def _():
        _gather_start(dest_next_ref, ys_ref, buf, sem, 1 - slot, tm, slabs)

    for k in range(TOP_K):
        pltpu.make_async_copy(ys_ref.at[pl.ds(0, tm * slabs)], buf.at[slot, k], sem.at[slot]).wait()
    y = None
    for k in range(TOP_K):
        rows = _load_row_slabs(buf, slabs, slot, k)
        y = gate_ref[:, k:k + 1] * rows if y is None else y + gate_ref[:, k:k + 1] * rows
    return y


def _combine_conv_in_kernel(dest_ref, dest_next_ref, ys_ref, gate_ref, x_ref, mod_ref, modn_ref, gmix_ref,
                            win_ref, bin_ref, x2_ref, u_ref, buf, sem):
    y = _gather_combine(dest_ref, dest_next_ref, ys_ref, gate_ref, buf, sem)
    x2 = x_ref[...] + mod_ref[0, 5:6, :] * y
    x2_ref[...] = x2
    h = _rmsnorm_mod(x2, gmix_ref[...], modn_ref[0, 1:2, :], modn_ref[0, 0:1, :])
    p = _dot(h.astype(BF16), win_ref[...]) + bin_ref[...]
    d = x2.shape[1]
    u_ref[...] = p[:, :d] * jax.nn.sigmoid(p[:, d:])


def _combine_final_kernel(dest_ref, dest_next_ref, ys_ref, gate_ref, x_ref, mod_ref, gfin_ref, out_ref, buf, sem):
    y = _gather_combine(dest_ref, dest_next_ref, ys_ref, gate_ref, buf, sem)
    x4 = x_ref[...] + mod_ref[0, 5:6, :] * y
    out_ref[...] = x4 * lax.rsqrt(jnp.mean(x4 * x4, axis=-1, keepdims=True) + EPS) * gfin_ref[...]


def _combine_common(n, d, tm, tiles_per_batch):
    tok = lambda i: (i, 0)
    last = n // tm - 1
    return dict(
        dest=pl.BlockSpec((TOP_K * tm,), lambda i: (i,), memory_space=pltpu.SMEM),
        dest_next=pl.BlockSpec((TOP_K * tm,), lambda i: (jnp.minimum(i + 1, last),), memory_space=pltpu.SMEM),
        ys=pl.BlockSpec(memory_space=pl.ANY),
        gate=pl.BlockSpec((tm, 8), tok),
        tok=pl.BlockSpec((tm, d), tok),
        mod=pl.BlockSpec((1, 6, d), lambda i: (i // tiles_per_batch, 0, 0)),
        vec=pl.BlockSpec((1, d), lambda i: (0, 0)),
        scratch=[pltpu.VMEM((2, TOP_K, tm * (d // V7X_LANES), V7X_LANES), F32), pltpu.SemaphoreType.DMA((2,))],
    )


def _combine_conv_in(dest_tiles, ys, gates, x, mods, mods_next, g_mix, w_in, b_in, tm):
    n, d = x.shape
    s = _combine_common(n, d, tm, n // mods.shape[0] // tm)
    return pl.pallas_call(
        _combine_conv_in_kernel,
        grid=(n // tm,),
        in_specs=[s["dest"], s["dest_next"], s["ys"], s["gate"], s["tok"], s["mod"], s["mod"], s["vec"],
                  pl.BlockSpec((d, 2 * d), lambda i: (0, 0)),
                  pl.BlockSpec((1, 2 * d), lambda i: (0, 0))],
        out_specs=[s["tok"], s["tok"]],
        out_shape=[jax.ShapeDtypeStruct((n, d), F32), jax.ShapeDtypeStruct((n, d), F32)],
        scratch_shapes=s["scratch"],
        compiler_params=_params(("arbitrary",), 48 << 20),
        name="moe_combine_conv_in",
    )(dest_tiles, dest_tiles, ys, gates, x, mods, mods_next, g_mix, w_in, b_in)


def _combine_final(dest_tiles, ys, gates, x, mods, g_final, tm):
    n, d = x.shape
    s = _combine_common(n, d, tm, n // mods.shape[0] // tm)
    return pl.pallas_call(
        _combine_final_kernel,
        grid=(n // tm,),
        in_specs=[s["dest"], s["dest_next"], s["ys"], s["gate"], s["tok"], s["mod"], s["vec"]],
        out_specs=s["tok"],
        out_shape=jax.ShapeDtypeStruct((n, d), F32),
        scratch_shapes=s["scratch"],
        compiler_params=_params(("arbitrary",), 32 << 20),
        name="moe_combine_final",
    )(dest_tiles, dest_tiles, ys, gates, x, mods, g_final)


def _moe_experts(route, counts, xs, layer, w1, b1, w2, b2, tmb, tm):
    n = route.shape[1]
    n_exp = w1.shape[1]
    slabs = w1.shape[2] // V7X_LANES
    expert_stride = xs.shape[0] // (n_exp * slabs)
    n_blocks = (n * TOP_K + n_exp * (tmb - 1) + tmb - 1) // tmb
    dest_tiles, blk_e, blk_row, n_valid = _routing_plan(route, counts, n_blocks, tmb, tm, expert_stride, slabs)
    ys = _experts(blk_e, blk_row, n_valid, xs, layer, w1, b1, w2, b2, n_blocks, tmb)
    return dest_tiles, ys


def _dwconv_kernel(cur_ref, prev_ref, next_ref, w_ref, bdw_ref, gln_ref, bln_ref, out_ref, hbuf, vbuf, zbuf):
    i = pl.program_id(1)
    rt, w, d = cur_ref.shape[1], cur_ref.shape[2], cur_ref.shape[3]
    half = d // 2
    pad = CONV_WIDTH // 2
    n_lg = half // V7X_LANES

    for r in range(rt):
        base = r * HCONV_PITCH
        for lg in range(n_lg):
            hbuf[lg, base:base + HCONV_MARGIN, :] = jnp.zeros((HCONV_MARGIN, V7X_LANES), F32)
            hbuf[lg, base + HCONV_MARGIN:base + HCONV_MARGIN + w, :] = cur_ref[0, r, :, lg * V7X_LANES:(lg + 1) * V7X_LANES]
            hbuf[lg, base + HCONV_MARGIN + w:base + HCONV_PITCH, :] = jnp.zeros(
                (HCONV_PITCH - HCONV_MARGIN - w, V7X_LANES), F32)
    vbuf[0:rt] = jnp.where(i > 0, prev_ref[0], 0.0)
    vbuf[rt:2 * rt] = cur_ref[0, :, :, half:d]
    vbuf[2 * rt:3 * rt] = jnp.where(i < pl.num_programs(1) - 1, next_ref[0], 0.0)

    def col_body(wi, carry):
        for lg in range(n_lg):
            ls = slice(lg * V7X_LANES, (lg + 1) * V7X_LANES)
            acc = jnp.zeros((rt, V7X_LANES), F32)
            for j in range(CONV_WIDTH):
                taps = hbuf[lg, pl.ds(wi + (HCONV_MARGIN - pad + j), rt, stride=HCONV_PITCH), :]
                acc = acc + taps * w_ref[j:j + 1, ls]
            zbuf[lg, pl.ds(wi, rt, stride=w), :] = acc
        return carry

    lax.fori_loop(0, w, col_body, 0)

    def row_body(r, carry):
        def piece_body(p, carry_p):
            w0 = pl.multiple_of(p * VCONV_PIECE, VCONV_PIECE)
            for lg in range(n_lg):
                ls = slice(lg * V7X_LANES, (lg + 1) * V7X_LANES)
                ls_hi = slice(half + lg * V7X_LANES, half + (lg + 1) * V7X_LANES)
                acc = jnp.zeros((VCONV_PIECE, V7X_LANES), F32)
                for j in range(CONV_WIDTH):
                    acc = acc + vbuf[rt - pad + j + r, pl.ds(w0, VCONV_PIECE), ls] * w_ref[j:j + 1, ls_hi]
                zbuf[n_lg + lg, pl.ds(pl.multiple_of(r * w + w0, VCONV_PIECE), VCONV_PIECE), :] = acc
            return carry_p

        return lax.fori_loop(0, w // VCONV_PIECE, piece_body, carry)

    lax.fori_loop(0, rt, row_body, 0)

    z = jnp.concatenate([zbuf[s] for s in range(d // V7X_LANES)], axis=1) + bdw_ref[...]
    mu = jnp.mean(z, axis=-1, keepdims=True)
    zc = z - mu
    y = zc * lax.rsqrt(jnp.mean(zc * zc, axis=-1, keepdims=True) + EPS) * gln_ref[...] + bln_ref[...]
    out_ref[0] = (y * jax.nn.sigmoid(y)).astype(BF16).reshape(rt, w, d)


def _dwconv(u, w_dw, b_dw, g_ln, b_ln, rt):
    b, r, w, d = u.shape
    half = d // 2
    nb = r // rt
    vec = pl.BlockSpec((1, d), lambda bi, i: (0, 0))
    return pl.pallas_call(
        _dwconv_kernel,
        grid=(b, nb),
        in_specs=[pl.BlockSpec((1, rt, w, d), lambda bi, i: (bi, i, 0, 0)),
                  pl.BlockSpec((1, rt, w, half), lambda bi, i: (bi, jnp.maximum(i - 1, 0), 0, 1)),
                  pl.BlockSpec((1, rt, w, half), lambda bi, i: (bi, jnp.minimum(i + 1, nb - 1), 0, 1)),
                  pl.BlockSpec((CONV_WIDTH, d), lambda bi, i: (0, 0)),
                  vec, vec, vec],
        out_specs=pl.BlockSpec((1, rt, w, d), lambda bi, i: (bi, i, 0, 0)),
        out_shape=jax.ShapeDtypeStruct((b, r, w, d), BF16),
        scratch_shapes=[pltpu.VMEM((half // V7X_LANES, rt * HCONV_PITCH, V7X_LANES), F32),
                        pltpu.VMEM((3 * rt, w, half), F32),
                        pltpu.VMEM((d // V7X_LANES, rt * w, V7X_LANES), F32)],
        compiler_params=_params(("arbitrary", "arbitrary"), 48 << 20),
        name="dwconv_ln_silu",
    )(u, u, u, w_dw, b_dw, g_ln, b_ln)


def kernel(x, c, ctx, c_ctx, w_ada, b_ada, g_mix, g_ffn, w_hgrn_in, hgrn_gamma, g_hgrn_out, w_hgrn_out,
           w_cv_in, b_cv_in, w_cv_dw, b_cv_dw, g_cv_ln, b_cv_ln, w_cv_out, b_cv_out,
           w_router, b_router, w_exp_in, b_exp_in, w_exp_out, b_exp_out, g_final):
    bsz, seq, d = x.shape
    n = bsz * seq
    n_heads = d // HEAD_DIM
    n_exp = w_router.shape[-1]
    assert bsz + 1 <= 8 and seq % 512 == 0 and ctx.shape[1] % 256 == 0

    cond = jnp.zeros((8, d), F32).at[:bsz].set(c).at[bsz].set(c_ctx)
    mods = _ada_mods(cond, w_ada, b_ada).reshape(w_ada.shape[0], 8, 6, d)
    mods_lat = [mods[l, :bsz] for l in range(w_ada.shape[0])]
    mods_ctx = jnp.broadcast_to(mods[0, bsz:bsz + 1], (bsz, 6, d))

    lb_all = jnp.cumsum(jax.nn.softmax(hgrn_gamma.astype(F32), axis=1), axis=1)
    lb = lb_all[:, 0, :]

    row = lambda v: v.reshape(1, -1)
    wr_t = [w_router[l].T.astype(BF16) for l in range(2)]
    br = [jnp.broadcast_to(b_router[l][:, None], (n_exp, V7X_LANES)) for l in range(2)]

    w_in = w_hgrn_in[0].astype(BF16)
    scan_t = 256
    zeros_state = jnp.zeros((bsz, n_heads, HEAD_DIM, HEAD_DIM), F32)
    pc = _hgrn_proj(ctx, mods_ctx, row(g_mix[0]), lb, w_in, 256)
    _, _, s_f, s_b = _gla_scan(pc[0], pc[1], pc[2], pc[3], pc[4], pc[5], zeros_state, zeros_state, scan_t, n_heads)
    pq, pkf, pkb, pv, pbf, pcb, pg = _hgrn_proj(x, mods_lat[0], row(g_mix[0]), lb, w_in, 256)
    o_f, o_b, _, _ = _gla_scan(pq, pkf, pkb, pv, pbf, pcb, s_f, s_b, scan_t, n_heads)

    tm_tail, tmb, tm_rows = 512, 512, 256
    x1, xs, route, gates, cnt = _hgrn_out(
        o_f.reshape(n, d), o_b.reshape(n, d), pg.reshape(n, d), x.reshape(n, d), mods_lat[0],
        row(g_hgrn_out[0]), w_hgrn_out[0].astype(BF16), row(g_ffn[0]), wr_t[0], br[0], tm_tail, tmb)
    dest_tiles, ys = _moe_experts(route, cnt[:, 0].astype(I32), xs, 0,
                                  w_exp_in, b_exp_in, w_exp_out, b_exp_out, tmb, tm_rows)

    x2, u = _combine_conv_in(dest_tiles, ys, gates.T, x1, mods_lat[0], mods_lat[1], row(g_mix[1]),
                             w_cv_in[0].astype(BF16), row(b_cv_in[0]), tm_rows)
    za = _dwconv(u.reshape(bsz, seq // GRID_W, GRID_W, d), w_cv_dw[0], row(b_cv_dw[0]),
                 row(g_cv_ln[0]), row(b_cv_ln[0]), 16)
    x3, xs, route, gates, cnt = _conv_out(
        za.reshape(n, d), x2, mods_lat[1], w_cv_out[0].astype(BF16), row(b_cv_out[0]),
        row(g_ffn[1]), wr_t[1], br[1], tm_tail, tmb)
    dest_tiles, ys = _moe_experts(route, cnt[:, 0].astype(I32), xs, 1,
                                  w_exp_in, b_exp_in, w_exp_out, b_exp_out, tmb, tm_rows)
    out = _combine_final(dest_tiles, ys, gates.T, x3, mods_lat[1], row(g_final), tm_rows)
    return out.reshape(bsz, seq, d)
```

```python
import functools

import jax
import jax.numpy as jnp
from jax import lax
from jax.experimental import pallas as pl
from jax.experimental.pallas import tpu as pltpu

F32 = jnp.float32
BF16 = jnp.bfloat16
I32 = jnp.int32
U32 = jnp.uint32

EPS = 1e-6
HEAD_DIM = 128
GLA_CHUNK = 128
GLA_HALF_DECAY_LIMIT = 80.0
GLA_DECAY_LIMIT = 60.0
DMA_ISSUE_UNROLL = 8
DMA_THREADS = 2
GRID_W = 64
CONV_WIDTH = 31
VCONV_PIECE = 16
HCONV_MARGIN = 16
HCONV_PITCH = 100
TOP_K = 4
SWIGLU_ALPHA = 1.702
SWIGLU_LIMIT = 7.0

V7X_LANES = 128
V7X_VMEM_BYTES = 64 * 1024 * 1024
VMEM_HEADROOM_BYTES = 8 * 1024 * 1024


def _params(semantics, vmem_bytes):
    limit = min(int(vmem_bytes), V7X_VMEM_BYTES - VMEM_HEADROOM_BYTES)
    return pltpu.CompilerParams(dimension_semantics=semantics, vmem_limit_bytes=limit)


def _dot(a, b):
    return jnp.dot(a, b, preferred_element_type=F32)


def _dot_nt(a, b):
    return lax.dot_general(a, b, (((1,), (1,)), ((), ())), preferred_element_type=F32)


def _dot_tn(a, b):
    return lax.dot_general(a, b, (((0,), (0,)), ((), ())), preferred_element_type=F32)


def _rmsnorm_mod(x, g, scale, shift):
    y = x * lax.rsqrt(jnp.mean(x * x, axis=-1, keepdims=True) + EPS)
    return y * g * (1.0 + scale) + shift


def _store_row_slabs(ref, x, *lead):
    m, d = x.shape
    slabs = d // V7X_LANES
    for s in range(slabs):
        ref[(*lead, pl.ds(s, m, stride=slabs), slice(None))] = x[:, s * V7X_LANES:(s + 1) * V7X_LANES]


def _load_row_slabs(ref, slabs, *lead):
    m = ref.shape[-2] // slabs
    return jnp.concatenate([ref[(*lead, pl.ds(s, m, stride=slabs), slice(None))] for s in range(slabs)], axis=1)


def _slab(row, slabs):
    return pl.ds(pl.multiple_of(row * slabs, slabs), slabs)


def _ada_kernel(c_ref, w_ref, b_ref, o_ref):
    s = c_ref[...]
    s = s * jax.nn.sigmoid(s)
    o_ref[0] = jnp.dot(s, w_ref[0], preferred_element_type=F32,
                       precision=lax.Precision.HIGHEST) + b_ref[0]


def _ada_mods(cond, w_ada, b_ada):
    depth, d, d6 = w_ada.shape
    rows = cond.shape[0]
    tn = 1536
    return pl.pallas_call(
        _ada_kernel,
        grid=(depth, d6 // tn),
        in_specs=[pl.BlockSpec((rows, d), lambda l, n: (0, 0)),
                  pl.BlockSpec((1, d, tn), lambda l, n: (l, 0, n)),
                  pl.BlockSpec((1, 1, tn), lambda l, n: (l, 0, n))],
        out_specs=pl.BlockSpec((1, rows, tn), lambda l, n: (l, 0, n)),
        out_shape=jax.ShapeDtypeStruct((depth, rows, d6), F32),
        compiler_params=_params(("arbitrary", "arbitrary"), 24 << 20),
        name="ada_mods",
    )(cond, w_ada, b_ada.reshape(depth, 1, d6))


def _chunk_cumsum(x, chunk, reverse):
    rows, cols = x.shape
    pos = lax.broadcasted_iota(I32, (rows, V7X_LANES), 0) & (chunk - 1)
    step = 1
    while step < chunk:
        keep = (pos < chunk - step) if reverse else (pos >= step)
        shift = rows - step if reverse else step
        parts = []
        for c in range(cols // V7X_LANES):
            xs = x[:, c * V7X_LANES:(c + 1) * V7X_LANES]
            parts.append(xs + jnp.where(keep, pltpu.roll(xs, shift, 0), 0.0))
        x = jnp.concatenate(parts, axis=1)
        step *= 2
    return x


def _hgrn_proj_kernel(x_ref, mod_ref, gmix_ref, lb_ref, w_ref,
                      q_ref, kf_ref, kb_ref, v_ref, bf_ref, cb_ref, g_ref):
    d = x_ref.shape[-1]
    h = _rmsnorm_mod(x_ref[0], gmix_ref[...], mod_ref[0, 1:2, :], mod_ref[0, 0:1, :])
    p = _dot(h.astype(BF16), w_ref[...])
    q_ref[0] = p[:, 0:d].astype(BF16)
    v_ref[0] = p[:, d:2 * d].astype(BF16)
    g_ref[0] = p[:, 4 * d:5 * d].astype(BF16)
    lbf = lb_ref[0:1, :]
    lbb = lb_ref[1:2, :]
    ff = lbf + (1.0 - lbf) * jax.nn.sigmoid(p[:, 2 * d:3 * d])
    fb = lbb + (1.0 - lbb) * jax.nn.sigmoid(p[:, 3 * d:4 * d])
    kf_ref[0] = (1.0 - ff).astype(BF16)
    kb_ref[0] = (1.0 - fb).astype(BF16)
    bf_ref[0] = _chunk_cumsum(jnp.log(ff), GLA_CHUNK, reverse=False)
    cb_ref[0] = _chunk_cumsum(jnp.log(fb), GLA_CHUNK, reverse=True)


def _hgrn_proj(x, mods, g_mix, lb, w_in, tm):
    b, l, d = x.shape
    d5 = w_in.shape[1]
    tok = lambda bi, i: (bi, i, 0)
    const2 = lambda bi, i: (0, 0)
    bf_sd = jax.ShapeDtypeStruct((b, l, d), BF16)
    f_sd = jax.ShapeDtypeStruct((b, l, d), F32)
    return pl.pallas_call(
        _hgrn_proj_kernel,
        grid=(b, l // tm),
        in_specs=[pl.BlockSpec((1, tm, d), tok),
                  pl.BlockSpec((1, 6, d), lambda bi, i: (bi, 0, 0)),
                  pl.BlockSpec((1, d), const2),
                  pl.BlockSpec((2, d), const2),
                  pl.BlockSpec((d, d5), const2)],
        out_specs=[pl.BlockSpec((1, tm, d), tok)] * 7,
        out_shape=[bf_sd, bf_sd, bf_sd, bf_sd, f_sd, f_sd, bf_sd],
        compiler_params=_params(("arbitrary", "arbitrary"), 52 << 20),
        name="hgrn_proj",
    )(x, mods, g_mix, lb, w_in)


def _gla_reference_rows(row_at, c, reverse):
    h, qt = c // 2, c // 4
    if reverse:
        return row_at(qt), row_at(h + qt), row_at(h)
    return row_at(qt - 1), row_at(h + qt - 1), row_at(h - 1)


def _gla_quarter_decay(row_at, c, reverse):
    h = c // 2
    m_lo, m_hi, _ = _gla_reference_rows(row_at, c, reverse)
    return jnp.maximum(jnp.maximum(jnp.abs(row_at(0) - m_lo), jnp.abs(row_at(h - 1) - m_lo)),
                       jnp.maximum(jnp.abs(row_at(h) - m_hi), jnp.abs(row_at(c - 1) - m_hi)))


def _gla_chunk(q, k, v, cum, st, *, reverse, mode):
    c = q.shape[0]
    qf = q.astype(F32)
    kf = k.astype(F32)
    last = cum[0:1] if reverse else cum[c - 1:c]
    row = lax.broadcasted_iota(I32, (c, c), 0)
    col = lax.broadcasted_iota(I32, (c, c), 1)
    if mode == "one_reference":
        mid_row = c // 2 if reverse else c // 2 - 1
        mid = cum[mid_row:mid_row + 1]
        x = cum - mid
        qe = qf * jnp.exp(x)
        ke = kf * jnp.exp(-x)
        a = _dot_nt(qe.astype(BF16), ke.astype(BF16))
        a = jnp.where((col >= row) if reverse else (row >= col), a, 0.0)
        qh = (qe * jnp.exp(mid)).astype(BF16)
        kh = (ke * jnp.exp(last - mid)).astype(BF16)
    elif mode == "three_references":
        m_lo, m_hi, bnd = _gla_reference_rows(lambda r: cum[r:r + 1], c, reverse)
        first = lax.broadcasted_iota(I32, cum.shape, 0) < c // 2
        x_own = cum - jnp.where(first, m_lo, m_hi)
        x_bnd = cum - bnd
        q_own = qf * jnp.exp(x_own)
        k_own = kf * jnp.exp(-x_own)
        q_bnd = qf * jnp.exp(x_bnd)
        k_bnd = kf * jnp.exp(-x_bnd)
        q_cross = jnp.where(first, q_bnd, 0.0) if reverse else jnp.where(first, 0.0, q_bnd)
        k_cross = jnp.where(first, 0.0, k_bnd) if reverse else jnp.where(first, k_bnd, 0.0)
        qe = jnp.concatenate([jnp.where(first, q_own, 0.0), jnp.where(first, 0.0, q_own), q_cross], axis=1)
        ke = jnp.concatenate([jnp.where(first, k_own, 0.0), jnp.where(first, 0.0, k_own), k_cross], axis=1)
        a = _dot_nt(qe.astype(BF16), ke.astype(BF16))
        a = jnp.where((col >= row) if reverse else (row >= col), a, 0.0)
        qh = (qf * jnp.exp(cum)).astype(BF16)
        kh = (kf * jnp.exp(last - cum)).astype(BF16)
    else:
        a = jnp.where(row == col, _dot_nt(q, k), 0.0)
        c1 = cum.astype(BF16)
        r1 = cum - c1.astype(F32)
        c2 = r1.astype(BF16)
        c3 = (r1 - c2.astype(F32)).astype(BF16)
        level = 0
        while (1 << level) < c:
            m = 1 << level
            seg_row = row >> (level + 1)
            ref_row = (seg_row << (level + 1)) + (m if reverse else m - 1)
            sel = jnp.where(col == ref_row, 1.0, 0.0).astype(BF16)
            cref = _dot(sel, c1) + _dot(sel, c2) + _dot(sel, c3)
            e = jnp.exp(-jnp.abs(cum - cref))
            contrib = _dot_nt((qf * e).astype(BF16), (kf * e).astype(BF16))
            row_bit = (row >> level) & 1
            col_bit = (col >> level) & 1
            same = seg_row == (col >> (level + 1))
            if reverse:
                pick = jnp.where(same, (1 - row_bit) * col_bit, 0)
            else:
                pick = jnp.where(same, row_bit * (1 - col_bit), 0)
            a = a + jnp.where(pick == 1, contrib, 0.0)
            level += 1
        qh = (qf * jnp.exp(cum)).astype(BF16)
        kh = (kf * jnp.exp(last - cum)).astype(BF16)
    o = _dot(a.astype(BF16), v) + _dot_nt(qh, st.astype(BF16))
    st_new = st * jnp.exp(last) + _dot_tn(v, kh)
    return o, st_new


def _gla_scan_kernel(qf_ref, kf_ref, vf_ref, bf_ref, qb_ref, kb_ref, vb_ref, cb_ref, s0f_ref, s0b_ref,
                     of_ref, ob_ref, sf_ref, sb_ref):
    j = pl.program_id(2)

    @pl.when(j == 0)
    def _():
        sf_ref[...] = s0f_ref[...]
        sb_ref[...] = s0b_ref[...]

    t = qf_ref.shape[1]
    n_chunks = t // GLA_CHUNK
    n_heads = qf_ref.shape[2] // HEAD_DIM

    worst_half = jnp.zeros((1, qf_ref.shape[2]), F32)
    worst_quarter = jnp.zeros((1, qf_ref.shape[2]), F32)
    for c in range(n_chunks):
        lo = c * GLA_CHUNK
        for cum_ref, reverse in ((bf_ref, False), (cb_ref, True)):
            row_at = lambda r, cum_ref=cum_ref, lo=lo: cum_ref[0, lo + r:lo + r + 1, :]
            mid = row_at(GLA_CHUNK // 2 if reverse else GLA_CHUNK // 2 - 1)
            worst_half = jnp.maximum(worst_half, jnp.maximum(jnp.abs(row_at(0) - mid),
                                                             jnp.abs(row_at(GLA_CHUNK - 1) - mid)))
            worst_quarter = jnp.maximum(worst_quarter, _gla_quarter_decay(row_at, GLA_CHUNK, reverse))
    mild = jnp.max(worst_half) <= GLA_HALF_DECAY_LIMIT
    medium = jnp.logical_and(jnp.logical_not(mild), jnp.max(worst_quarter) <= GLA_DECAY_LIMIT)
    steep = jnp.logical_not(jnp.logical_or(mild, medium))

    directions = ((qf_ref, kf_ref, vf_ref, bf_ref, of_ref, sf_ref, False),
                  (qb_ref, kb_ref, vb_ref, cb_ref, ob_ref, sb_ref, True))

    def chunk_step(refs, h, c, st, mode):
        q_ref, k_ref, v_ref, cum_ref, o_ref, _, reverse = refs
        ls = slice(h * HEAD_DIM, (h + 1) * HEAD_DIM)
        start = c * GLA_CHUNK
        rs = pl.ds(start if isinstance(c, int) else pl.multiple_of(start, GLA_CHUNK), GLA_CHUNK)
        o, st = _gla_chunk(q_ref[0, rs, ls], k_ref[0, rs, ls], v_ref[0, rs, ls], cum_ref[0, rs, ls],
                           st, reverse=reverse, mode=mode)
        o_ref[0, rs, ls] = o
        return st

    def run_looped(mode):
        for refs in directions:
            for h in range(n_heads):
                def body(i, st, refs=refs, h=h):
                    c = n_chunks - 1 - i if refs[6] else i
                    return chunk_step(refs, h, c, st, mode)

                refs[5][0, h] = lax.fori_loop(0, n_chunks, body, refs[5][0, h])

    @pl.when(steep)
    def _():
        run_looped("dyadic")

    @pl.when(medium)
    def _():
        run_looped("three_references")

    @pl.when(mild)
    def _():
        for h in range(n_heads):
            for refs in directions:
                st = refs[5][0, h]
                for i in range(n_chunks):
                    st = chunk_step(refs, h, n_chunks - 1 - i if refs[6] else i, st, "one_reference")
                refs[5][0, h] = st


def _gla_scan(q, kf, kb, v, bf, cb, s0f, s0b, t, heads_per_step):
    b, l, d = q.shape
    lanes = heads_per_step * HEAD_DIM
    nblk = l // t
    fwd = lambda bi, hp, j: (bi, j, hp)
    bwd = lambda bi, hp, j: (bi, nblk - 1 - j, hp)
    st_map = lambda bi, hp, j: (bi, hp, 0, 0)
    tok_f = pl.BlockSpec((1, t, lanes), fwd)
    tok_b = pl.BlockSpec((1, t, lanes), bwd)
    st_spec = pl.BlockSpec((1, heads_per_step, HEAD_DIM, HEAD_DIM), st_map)
    o_sd = jax.ShapeDtypeStruct((b, l, d), F32)
    s_sd = jax.ShapeDtypeStruct(s0f.shape, F32)
    return pl.pallas_call(
        _gla_scan_kernel,
        grid=(b, d // lanes, nblk),
        in_specs=[tok_f, tok_f, tok_f, tok_f, tok_b, tok_b, tok_b, tok_b, st_spec, st_spec],
        out_specs=[tok_f, tok_b, st_spec, st_spec],
        out_shape=[o_sd, o_sd, s_sd, s_sd],
        compiler_params=_params(("arbitrary", "arbitrary", "arbitrary"), 32 << 20),
        name="gla_scan",
    )(q, kf, v, bf, q, kb, v, cb, s0f, s0b)


def _ffn_route_tail(x, y, mod_ref, gffn_ref, wr_ref, br_ref, x1_ref, xs_ref, route_ref, gate_ref, cnt_ref,
                    stage, dest_vm, dest_sm, cnt_vm, cnt_sm, zero_buf, sem, *, expert_stride, fill_rows):
    tm, d = x.shape
    slabs = d // V7X_LANES
    i = pl.program_id(0)
    par = lax.rem(i, 2)
    x1 = x + mod_ref[0, 2:3, :] * y
    x1_ref[...] = x1
    hf = _rmsnorm_mod(x1, gffn_ref[...], mod_ref[0, 4:5, :], mod_ref[0, 3:4, :])
    _store_row_slabs(stage, hf, par)

    logits = _dot_nt(wr_ref[...], hf.astype(BF16)) + br_ref[:, 0:1]
    n_exp = logits.shape[0]
    eio = lax.broadcasted_iota(I32, (n_exp, tm), 0)
    vals, idxs, hots = [], [], []
    rest = logits
    for _ in range(TOP_K):
        m = jnp.max(rest, axis=0, keepdims=True)
        idx = jnp.min(jnp.where(rest == m, eio, n_exp), axis=0, keepdims=True)
        hot = eio == idx
        rest = jnp.where(hot, -jnp.inf, rest)
        vals.append(m)
        idxs.append(idx)
        hots.append(hot)
    ex = [jnp.exp(vk - vals[0]) for vk in vals]
    den = ex[0] + ex[1] + ex[2] + ex[3]
    gates = [e / den for e in ex]

    picked = jnp.zeros((n_exp, tm), F32)
    for hot in hots:
        picked = picked + jnp.where(hot, 1.0, 0.0)
    earlier = lax.broadcasted_iota(I32, (tm, tm), 0) < lax.broadcasted_iota(I32, (tm, tm), 1)
    before = _dot(picked.astype(BF16), jnp.where(earlier, 1.0, 0.0).astype(BF16)) + cnt_ref[:, 0:1]
    ranks = [jnp.sum(jnp.where(hot, before, 0.0), axis=0, keepdims=True).astype(I32) for hot in hots]
    cnt_ref[...] = cnt_ref[...] + jnp.sum(picked, axis=1, keepdims=True)

    route_ref[...] = jnp.concatenate(idxs + ranks, axis=0)
    gate_ref[...] = jnp.concatenate(gates + [jnp.zeros_like(g) for g in gates], axis=0)

    dests = [(idx * expert_stride + rank) * slabs for idx, rank in zip(idxs, ranks)]
    dest_vm[...] = jnp.concatenate(dests + [jnp.zeros_like(dd) for dd in dests], axis=0)
    to_smem = pltpu.make_async_copy(dest_vm, dest_sm, sem.at[2])
    to_smem.start()
    to_smem.wait()

    def issue(r, carry):
        for k in range(TOP_K):
            pltpu.make_async_copy(stage.at[par, _slab(r, slabs)],
                                  xs_ref.at[pl.ds(pl.multiple_of(dest_sm[k, r], slabs), slabs)],
                                  sem.at[par]).start(priority=k % DMA_THREADS)
        return carry

    lax.fori_loop(0, tm, issue, 0, unroll=DMA_ISSUE_UNROLL)

    def wait_rows(parity):
        for k in range(TOP_K):
            pltpu.make_async_copy(stage.at[0], xs_ref.at[pl.ds(0, tm * slabs)], sem.at[parity]).wait()

    @pl.when(i > 0)
    def _():
        wait_rows(1 - par)

    @pl.when(i == pl.num_programs(0) - 1)
    def _():
        wait_rows(par)
        zero_buf[...] = jnp.zeros_like(zero_buf)
        cnt_vm[...] = cnt_ref[...].astype(I32)
        cnt_copy = pltpu.make_async_copy(cnt_vm, cnt_sm, sem.at[2])
        cnt_copy.start()
        cnt_copy.wait()

        def pad_rows(e):
            first = e * expert_stride + cnt_sm[e, 0]
            return xs_ref.at[pl.ds(pl.multiple_of(first * slabs, slabs), fill_rows * slabs)]

        for e in range(cnt_sm.shape[0]):
            pltpu.make_async_copy(zero_buf, pad_rows(e), sem.at[2]).start()
        for e in range(cnt_sm.shape[0]):
            pltpu.make_async_copy(zero_buf, pad_rows(e), sem.at[2]).wait()


def _hgrn_out_kernel(of_ref, ob_ref, g_ref, x_ref, mod_ref, gout_ref, wout_ref, gffn_ref, wr_ref, br_ref,
                     x1_ref, xs_ref, route_ref, gate_ref, cnt_ref, *scratch, **dispatch):
    @pl.when(pl.program_id(0) == 0)
    def _():
        cnt_ref[...] = jnp.zeros_like(cnt_ref)

    o = of_ref[...] + ob_ref[...]
    parts = []
    for h in range(o.shape[1] // HEAD_DIM):
        oh = o[:, h * HEAD_DIM:(h + 1) * HEAD_DIM]
        parts.append(oh * lax.rsqrt(jnp.mean(oh * oh, axis=-1, keepdims=True) + EPS))
    on = jnp.concatenate(parts, axis=1) * gout_ref[...]
    g = g_ref[...].astype(F32)
    a = on * (g * jax.nn.sigmoid(g))
    y = _dot(a.astype(BF16), wout_ref[...])
    _ffn_route_tail(x_ref[...], y, mod_ref, gffn_ref, wr_ref, br_ref,
                    x1_ref, xs_ref, route_ref, gate_ref, cnt_ref, *scratch, **dispatch)


def _conv_out_kernel(a_ref, x_ref, mod_ref, wout_ref, bout_ref, gffn_ref, wr_ref, br_ref,
                     x1_ref, xs_ref, route_ref, gate_ref, cnt_ref, *scratch, **dispatch):
    @pl.when(pl.program_id(0) == 0)
    def _():
        cnt_ref[...] = jnp.zeros_like(cnt_ref)

    y = _dot(a_ref[...], wout_ref[...]) + bout_ref[...]
    _ffn_route_tail(x_ref[...], y, mod_ref, gffn_ref, wr_ref, br_ref,
                    x1_ref, xs_ref, route_ref, gate_ref, cnt_ref, *scratch, **dispatch)


def _tail_specs(n, d, n_exp, tm, tiles_per_batch, fill_rows):
    tok = lambda i: (i, 0)
    const2 = lambda i: (0, 0)
    slabs = d // V7X_LANES
    expert_stride = n + fill_rows
    specs = dict(
        tok=pl.BlockSpec((tm, d), tok),
        mod=pl.BlockSpec((1, 6, d), lambda i: (i // tiles_per_batch, 0, 0)),
        vec=pl.BlockSpec((1, d), const2),
        mat=pl.BlockSpec((d, d), const2),
        wr=pl.BlockSpec((n_exp, d), const2),
        br=pl.BlockSpec((n_exp, V7X_LANES), const2),
    )
    out_specs = [pl.BlockSpec((tm, d), tok),
                 pl.BlockSpec(memory_space=pl.ANY),
                 pl.BlockSpec((8, tm), lambda i: (0, i)),
                 pl.BlockSpec((8, tm), lambda i: (0, i)),
                 pl.BlockSpec((n_exp, V7X_LANES), const2)]
    out_shape = [jax.ShapeDtypeStruct((n, d), F32),
                 jax.ShapeDtypeStruct((n_exp * expert_stride * slabs, V7X_LANES), F32),
                 jax.ShapeDtypeStruct((8, n), I32),
                 jax.ShapeDtypeStruct((8, n), F32),
                 jax.ShapeDtypeStruct((n_exp, V7X_LANES), F32)]
    scratch = [pltpu.VMEM((2, tm * slabs, V7X_LANES), F32),
               pltpu.VMEM((8, tm), I32), pltpu.SMEM((8, tm), I32),
               pltpu.VMEM((n_exp, V7X_LANES), I32), pltpu.SMEM((n_exp, V7X_LANES), I32),
               pltpu.VMEM((fill_rows * slabs, V7X_LANES), F32),
               pltpu.SemaphoreType.DMA((3,))]
    dispatch = dict(expert_stride=expert_stride, fill_rows=fill_rows)
    return specs, out_specs, out_shape, scratch, dispatch


def _hgrn_out(o_f, o_b, g, x, mods, g_out, w_out, g_ffn, wr_t, br, tm, fill_rows):
    n, d = x.shape
    s, out_specs, out_shape, scratch, dispatch = _tail_specs(
        n, d, wr_t.shape[0], tm, n // mods.shape[0] // tm, fill_rows)
    return pl.pallas_call(
        functools.partial(_hgrn_out_kernel, **dispatch),
        grid=(n // tm,),
        in_specs=[s["tok"], s["tok"], s["tok"], s["tok"], s["mod"], s["vec"], s["mat"], s["vec"], s["wr"], s["br"]],
        out_specs=out_specs,
        out_shape=out_shape,
        scratch_shapes=scratch,
        compiler_params=_params(("arbitrary",), 52 << 20),
        name="hgrn_out_route",
    )(o_f, o_b, g, x, mods, g_out, w_out, g_ffn, wr_t, br)


def _conv_out(a, x, mods, w_out, b_out, g_ffn, wr_t, br, tm, fill_rows):
    n, d = x.shape
    s, out_specs, out_shape, scratch, dispatch = _tail_specs(
        n, d, wr_t.shape[0], tm, n // mods.shape[0] // tm, fill_rows)
    return pl.pallas_call(
        functools.partial(_conv_out_kernel, **dispatch),
        grid=(n // tm,),
        in_specs=[s["tok"], s["tok"], s["mod"], s["mat"], s["vec"], s["vec"], s["wr"], s["br"]],
        out_specs=out_specs,
        out_shape=out_shape,
        scratch_shapes=scratch,
        compiler_params=_params(("arbitrary",), 44 << 20),
        name="conv_out_route",
    )(a, x, mods, w_out, b_out, g_ffn, wr_t, br)


def _routing_plan(route, counts, n_blocks, tmb, tm, expert_stride, slabs):
    n_exp = counts.shape[0]
    padded = (counts + tmb - 1) // tmb * tmb
    pend = jnp.cumsum(padded)
    pstart = pend - padded
    n_valid = pend[-1] // tmb
    blk = jnp.minimum(jnp.arange(n_blocks, dtype=I32), n_valid - 1)
    blk_e = jnp.minimum(jnp.sum((blk[:, None] * tmb >= pend[None, :]).astype(I32), axis=1), n_exp - 1)
    blk_in_e = blk - jnp.sum(jnp.where(blk_e[:, None] == jnp.arange(n_exp, dtype=I32)[None, :],
                                       (pstart // tmb)[None, :], 0), axis=1)
    blk_row = blk_e * (expert_stride // tmb) + blk_in_e
    eidx, rank = route[:TOP_K], route[TOP_K:2 * TOP_K]
    hit = eidx[None] == jnp.arange(n_exp, dtype=I32)[:, None, None]
    dest = (rank + jnp.sum(jnp.where(hit, pstart[:, None, None], 0), axis=0)) * slabs
    n = dest.shape[1]
    dest_tiles = dest.reshape(TOP_K, n // tm, tm).transpose(1, 0, 2).reshape(-1)
    return dest_tiles.astype(I32), blk_e.astype(I32), blk_row.astype(I32), n_valid.reshape(1).astype(I32)


def _expert_kernel(blk_e_ref, blk_row_ref, n_valid_ref, xs_ref, w1_ref, b1_ref, w2_ref, b2_ref, ys_ref, w1_bf, w2_bf):
    del blk_row_ref
    i = pl.program_id(0)
    live = i < n_valid_ref[0]
    new_expert = jnp.logical_or(i == 0, blk_e_ref[i] != blk_e_ref[jnp.maximum(i - 1, 0)])

    @pl.when(jnp.logical_and(live, new_expert))
    def _():
        w1_bf[...] = w1_ref[0, 0].astype(BF16)
        w2_bf[...] = w2_ref[0, 0].astype(BF16)

    @pl.when(live)
    def _():
        slabs = w1_bf.shape[0] // V7X_LANES
        h = _dot(_load_row_slabs(xs_ref, slabs).astype(BF16), w1_bf[...]) + b1_ref[0, 0]
        de = h.shape[1] // 2
        gate = jnp.minimum(h[:, :de], SWIGLU_LIMIT)
        up = jnp.clip(h[:, de:], -SWIGLU_LIMIT, SWIGLU_LIMIT)
        act = (up + 1.0) * gate * jax.nn.sigmoid(SWIGLU_ALPHA * gate)
        _store_row_slabs(ys_ref, _dot(act.astype(BF16), w2_bf[...]) + b2_ref[0, 0])

    @pl.when(jnp.logical_not(live))
    def _():
        ys_ref[...] = jnp.zeros_like(ys_ref)


def _experts(blk_e, blk_row, n_valid, xs, layer, w1, b1, w2, b2, n_blocks, tmb):
    _, n_exp, d, de2 = w1.shape
    slabs = d // V7X_LANES
    row_map = lambda i, be, br, nv: (br[i], 0)
    w_map = lambda i, be, br, nv: (layer, be[i], 0, 0)
    return pl.pallas_call(
        _expert_kernel,
        grid_spec=pltpu.PrefetchScalarGridSpec(
            num_scalar_prefetch=3,
            grid=(n_blocks,),
            in_specs=[pl.BlockSpec((tmb * slabs, V7X_LANES), row_map),
                      pl.BlockSpec((1, 1, d, de2), w_map),
                      pl.BlockSpec((1, 1, 1, de2), w_map),
                      pl.BlockSpec((1, 1, de2 // 2, d), w_map),
                      pl.BlockSpec((1, 1, 1, d), w_map)],
            out_specs=pl.BlockSpec((tmb * slabs, V7X_LANES), lambda i, be, br, nv: (i, 0)),
            scratch_shapes=[pltpu.VMEM((d, de2), BF16), pltpu.VMEM((de2 // 2, d), BF16)]),
        out_shape=jax.ShapeDtypeStruct((n_blocks * tmb * slabs, V7X_LANES), xs.dtype),
        compiler_params=_params(("arbitrary",), 56 << 20),
        name="moe_experts",
    )(blk_e, blk_row, n_valid, xs, w1, b1.reshape(-1, n_exp, 1, de2), w2, b2.reshape(-1, n_exp, 1, d))


def _gather_start(dest_ref, ys_ref, buf, sem, slot, tm, slabs):
    def issue(r, carry):
        for k in range(TOP_K):
            pltpu.make_async_copy(ys_ref.at[pl.ds(pl.multiple_of(dest_ref[k * tm + r], slabs), slabs)],
                                  buf.at[slot, k, _slab(r, slabs)], sem.at[slot]).start(priority=k % DMA_THREADS)
        return carry

    lax.fori_loop(0, tm, issue, 0, unroll=DMA_ISSUE_UNROLL)


def _gather_combine(dest_ref, dest_next_ref, ys_ref, gate_ref, buf, sem):
    i = pl.program_id(0)
    slot = lax.rem(i, 2)
    tm = gate_ref.shape[0]
    slabs = buf.shape[2] // tm

    @pl.when(i == 0)
    def _():
        _gather_start(dest_ref, ys_ref, buf, sem, 0, tm, slabs)

    @pl.when(i + 1 < pl.num_programs(0))
    def _():
        _gather_start(dest_next_ref, ys_ref, buf, sem, 1 - slot, tm, slabs)

    for k in range(TOP_K):
        pltpu.make_async_copy(ys_ref.at[pl.ds(0, tm * slabs)], buf.at[slot, k], sem.at[slot]).wait()
    y = None
    for k in range(TOP_K):
        rows = _load_row_slabs(buf, slabs, slot, k)
        y = gate_ref[:, k:k + 1] * rows if y is None else y + gate_ref[:, k:k + 1] * rows
    return y


def _combine_conv_in_kernel(dest_ref, dest_next_ref, ys_ref, gate_ref, x_ref, mod_ref, modn_ref, gmix_ref,
                            win_ref, bin_ref, x2_ref, u_ref, buf, sem):
    y = _gather_combine(dest_ref, dest_next_ref, ys_ref, gate_ref, buf, sem)
    x2 = x_ref[...] + mod_ref[0, 5:6, :] * y
    x2_ref[...] = x2
    h = _rmsnorm_mod(x2, gmix_ref[...], modn_ref[0, 1:2, :], modn_ref[0, 0:1, :])
    p = _dot(h.astype(BF16), win_ref[...]) + bin_ref[...]
    d = x2.shape[1]
    u_ref[...] = p[:, :d] * jax.nn.sigmoid(p[:, d:])


def _combine_final_kernel(dest_ref, dest_next_ref, ys_ref, gate_ref, x_ref, mod_ref, gfin_ref, out_ref, buf, sem):
    y = _gather_combine(dest_ref, dest_next_ref, ys_ref, gate_ref, buf, sem)
    x4 = x_ref[...] + mod_ref[0, 5:6, :] * y
    out_ref[...] = x4 * lax.rsqrt(jnp.mean(x4 * x4, axis=-1, keepdims=True) + EPS) * gfin_ref[...]


def _combine_common(n, d, tm, tiles_per_batch):
    tok = lambda i: (i, 0)
    last = n // tm - 1
    return dict(
        dest=pl.BlockSpec((TOP_K * tm,), lambda i: (i,), memory_space=pltpu.SMEM),
        dest_next=pl.BlockSpec((TOP_K * tm,), lambda i: (jnp.minimum(i + 1, last),), memory_space=pltpu.SMEM),
        ys=pl.BlockSpec(memory_space=pl.ANY),
        gate=pl.BlockSpec((tm, 8), tok),
        tok=pl.BlockSpec((tm, d), tok),
        mod=pl.BlockSpec((1, 6, d), lambda i: (i // tiles_per_batch, 0, 0)),
        vec=pl.BlockSpec((1, d), lambda i: (0, 0)),
        scratch=[pltpu.VMEM((2, TOP_K, tm * (d // V7X_LANES), V7X_LANES), F32), pltpu.SemaphoreType.DMA((2,))],
    )


def _combine_conv_in(dest_tiles, ys, gates, x, mods, mods_next, g_mix, w_in, b_in, tm):
    n, d = x.shape
    s = _combine_common(n, d, tm, n // mods.shape[0] // tm)
    return pl.pallas_call(
        _combine_conv_in_kernel,
        grid=(n // tm,),
        in_specs=[s["dest"], s["dest_next"], s["ys"], s["gate"], s["tok"], s["mod"], s["mod"], s["vec"],
                  pl.BlockSpec((d, 2 * d), lambda i: (0, 0)),
                  pl.BlockSpec((1, 2 * d), lambda i: (0, 0))],
        out_specs=[s["tok"], s["tok"]],
        out_shape=[jax.ShapeDtypeStruct((n, d), F32), jax.ShapeDtypeStruct((n, d), F32)],
        scratch_shapes=s["scratch"],
        compiler_params=_params(("arbitrary",), 48 << 20),
        name="moe_combine_conv_in",
    )(dest_tiles, dest_tiles, ys, gates, x, mods, mods_next, g_mix, w_in, b_in)


def _combine_final(dest_tiles, ys, gates, x, mods, g_final, tm):
    n, d = x.shape
    s = _combine_common(n, d, tm, n // mods.shape[0] // tm)
    return pl.pallas_call(
        _combine_final_kernel,
        grid=(n // tm,),
        in_specs=[s["dest"], s["dest_next"], s["ys"], s["gate"], s["tok"], s["mod"], s["vec"]],
        out_specs=s["tok"],
        out_shape=jax.ShapeDtypeStruct((n, d), F32),
        scratch_shapes=s["scratch"],
        compiler_params=_params(("arbitrary",), 32 << 20),
        name="moe_combine_final",
    )(dest_tiles, dest_tiles, ys, gates, x, mods, g_final)


def _moe_experts(route, counts, xs, layer, w1, b1, w2, b2, tmb, tm):
    n = route.shape[1]
    n_exp = w1.shape[1]
    slabs = w1.shape[2] // V7X_LANES
    expert_stride = xs.shape[0] // (n_exp * slabs)
    n_blocks = (n * TOP_K + n_exp * (tmb - 1) + tmb - 1) // tmb
    dest_tiles, blk_e, blk_row, n_valid = _routing_plan(route, counts, n_blocks, tmb, tm, expert_stride, slabs)
    ys = _experts(blk_e, blk_row, n_valid, xs, layer, w1, b1, w2, b2, n_blocks, tmb)
    return dest_tiles, ys


def _dwconv_kernel(cur_ref, prev_ref, next_ref, w_ref, bdw_ref, gln_ref, bln_ref, out_ref, hbuf, vbuf, zbuf):
    i = pl.program_id(1)
    rt, w, d = cur_ref.shape[1], cur_ref.shape[2], cur_ref.shape[3]
    half = d // 2
    pad = CONV_WIDTH // 2
    n_lg = half // V7X_LANES

    for r in range(rt):
        base = r * HCONV_PITCH
        for lg in range(n_lg):
            hbuf[lg, base:base + HCONV_MARGIN, :] = jnp.zeros((HCONV_MARGIN, V7X_LANES), F32)
            hbuf[lg, base + HCONV_MARGIN:base + HCONV_MARGIN + w, :] = cur_ref[0, r, :, lg * V7X_LANES:(lg + 1) * V7X_LANES]
            hbuf[lg, base + HCONV_MARGIN + w:base + HCONV_PITCH, :] = jnp.zeros(
                (HCONV_PITCH - HCONV_MARGIN - w, V7X_LANES), F32)
    vbuf[0:rt] = jnp.where(i > 0, prev_ref[0], 0.0)
    vbuf[rt:2 * rt] = cur_ref[0, :, :, half:d]
    vbuf[2 * rt:3 * rt] = jnp.where(i < pl.num_programs(1) - 1, next_ref[0], 0.0)

    def col_body(wi, carry):
        for lg in range(n_lg):
            ls = slice(lg * V7X_LANES, (lg + 1) * V7X_LANES)
            acc = jnp.zeros((rt, V7X_LANES), F32)
            for j in range(CONV_WIDTH):
                taps = hbuf[lg, pl.ds(wi + (HCONV_MARGIN - pad + j), rt, stride=HCONV_PITCH), :]
                acc = acc + taps * w_ref[j:j + 1, ls]
            zbuf[lg, pl.ds(wi, rt, stride=w), :] = acc
        return carry

    lax.fori_loop(0, w, col_body, 0)

    def row_body(r, carry):
        def piece_body(p, carry_p):
            w0 = pl.multiple_of(p * VCONV_PIECE, VCONV_PIECE)
            for lg in range(n_lg):
                ls = slice(lg * V7X_LANES, (lg + 1) * V7X_LANES)
                ls_hi = slice(half + lg * V7X_LANES, half + (lg + 1) * V7X_LANES)
                acc = jnp.zeros((VCONV_PIECE, V7X_LANES), F32)
                for j in range(CONV_WIDTH):
                    acc = acc + vbuf[rt - pad + j + r, pl.ds(w0, VCONV_PIECE), ls] * w_ref[j:j + 1, ls_hi]
                zbuf[n_lg + lg, pl.ds(pl.multiple_of(r * w + w0, VCONV_PIECE), VCONV_PIECE), :] = acc
            return carry_p

        return lax.fori_loop(0, w // VCONV_PIECE, piece_body, carry)

    lax.fori_loop(0, rt, row_body, 0)

    z = jnp.concatenate([zbuf[s] for s in range(d // V7X_LANES)], axis=1) + bdw_ref[...]
    mu = jnp.mean(z, axis=-1, keepdims=True)
    zc = z - mu
    y = zc * lax.rsqrt(jnp.mean(zc * zc, axis=-1, keepdims=True) + EPS) * gln_ref[...] + bln_ref[...]
    out_ref[0] = (y * jax.nn.sigmoid(y)).astype(BF16).reshape(rt, w, d)


def _dwconv(u, w_dw, b_dw, g_ln, b_ln, rt):
    b, r, w, d = u.shape
    half = d // 2
    nb = r // rt
    vec = pl.BlockSpec((1, d), lambda bi, i: (0, 0))
    return pl.pallas_call(
        _dwconv_kernel,
        grid=(b, nb),
        in_specs=[pl.BlockSpec((1, rt, w, d), lambda bi, i: (bi, i, 0, 0)),
                  pl.BlockSpec((1, rt, w, half), lambda bi, i: (bi, jnp.maximum(i - 1, 0), 0, 1)),
                  pl.BlockSpec((1, rt, w, half), lambda bi, i: (bi, jnp.minimum(i + 1, nb - 1), 0, 1)),
                  pl.BlockSpec((CONV_WIDTH, d), lambda bi, i: (0, 0)),
                  vec, vec, vec],
        out_specs=pl.BlockSpec((1, rt, w, d), lambda bi, i: (bi, i, 0, 0)),
        out_shape=jax.ShapeDtypeStruct((b, r, w, d), BF16),
        scratch_shapes=[pltpu.VMEM((half // V7X_LANES, rt * HCONV_PITCH, V7X_LANES), F32),
                        pltpu.VMEM((3 * rt, w, half), F32),
                        pltpu.VMEM((d // V7X_LANES, rt * w, V7X_LANES), F32)],
        compiler_params=_params(("arbitrary", "arbitrary"), 48 << 20),
        name="dwconv_ln_silu",
    )(u, u, u, w_dw, b_dw, g_ln, b_ln)


def kernel(x, c, ctx, c_ctx, w_ada, b_ada, g_mix, g_ffn, w_hgrn_in, hgrn_gamma, g_hgrn_out, w_hgrn_out,
           w_cv_in, b_cv_in, w_cv_dw, b_cv_dw, g_cv_ln, b_cv_ln, w_cv_out, b_cv_out,
           w_router, b_router, w_exp_in, b_exp_in, w_exp_out, b_exp_out, g_final):
    bsz, seq, d = x.shape
    n = bsz * seq
    n_heads = d // HEAD_DIM
    n_exp = w_router.shape[-1]
    assert bsz + 1 <= 8 and seq % 512 == 0 and ctx.shape[1] % 256 == 0

    cond = jnp.zeros((8, d), F32).at[:bsz].set(c).at[bsz].set(c_ctx)
    mods = _ada_mods(cond, w_ada, b_ada).reshape(w_ada.shape[0], 8, 6, d)
    mods_lat = [mods[l, :bsz] for l in range(w_ada.shape[0])]
    mods_ctx = jnp.broadcast_to(mods[0, bsz:bsz + 1], (bsz, 6, d))

    lb_all = jnp.cumsum(jax.nn.softmax(hgrn_gamma.astype(F32), axis=1), axis=1)
    lb = lb_all[:, 0, :]

    row = lambda v: v.reshape(1, -1)
    wr_t = [w_router[l].T.astype(BF16) for l in range(2)]
    br = [jnp.broadcast_to(b_router[l][:, None], (n_exp, V7X_LANES)) for l in range(2)]

    w_in = w_hgrn_in[0].astype(BF16)
    scan_t = 256
    zeros_state = jnp.zeros((bsz, n_heads, HEAD_DIM, HEAD_DIM), F32)
    pc = _hgrn_proj(ctx, mods_ctx, row(g_mix[0]), lb, w_in, 256)
    _, _, s_f, s_b = _gla_scan(pc[0], pc[1], pc[2], pc[3], pc[4], pc[5], zeros_state, zeros_state, scan_t, n_heads)
    pq, pkf, pkb, pv, pbf, pcb, pg = _hgrn_proj(x, mods_lat[0], row(g_mix[0]), lb, w_in, 256)
    o_f, o_b, _, _ = _gla_scan(pq, pkf, pkb, pv, pbf, pcb, s_f, s_b, scan_t, n_heads)

    tm_tail, tmb, tm_rows = 512, 512, 256
    x1, xs, route, gates, cnt = _hgrn_out(
        o_f.reshape(n, d), o_b.reshape(n, d), pg.reshape(n, d), x.reshape(n, d), mods_lat[0],
        row(g_hgrn_out[0]), w_hgrn_out[0].astype(BF16), row(g_ffn[0]), wr_t[0], br[0], tm_tail, tmb)
    dest_tiles, ys = _moe_experts(route, cnt[:, 0].astype(I32), xs, 0,
                                  w_exp_in, b_exp_in, w_exp_out, b_exp_out, tmb, tm_rows)

    x2, u = _combine_conv_in(dest_tiles, ys, gates.T, x1, mods_lat[0], mods_lat[1], row(g_mix[1]),
                             w_cv_in[0].astype(BF16), row(b_cv_in[0]), tm_rows)
    za = _dwconv(u.reshape(bsz, seq // GRID_W, GRID_W, d), w_cv_dw[0], row(b_cv_dw[0]),
                 row(g_cv_ln[0]), row(b_cv_ln[0]), 16)
    x3, xs, route, gates, cnt = _conv_out(
        za.reshape(n, d), x2, mods_lat[1], w_cv_out[0].astype(BF16), row(b_cv_out[0]),
        row(g_ffn[1]), wr_t[1], br[1], tm_tail, tmb)
    dest_tiles, ys = _moe_experts(route, cnt[:, 0].astype(I32), xs, 1,
                                  w_exp_in, b_exp_in, w_exp_out, b_exp_out, tmb, tm_rows)
    out = _combine_final(dest_tiles, ys, gates.T, x3, mods_lat[1], row(g_final), tm_rows)
    return out.reshape(bsz, seq, d)
```

```python
import functools

import jax
import jax.numpy as jnp
from jax import lax
from jax.experimental import pallas as pl
from jax.experimental.pallas import tpu as pltpu

F32 = jnp.float32
BF16 = jnp.bfloat16
I32 = jnp.int32
U32 = jnp.uint32

EPS = 1e-6
HEAD_DIM = 128
GLA_CHUNK = 128
GLA_HALF_DECAY_LIMIT = 80.0
GLA_DECAY_LIMIT = 60.0
DMA_ISSUE_UNROLL = 8
DMA_THREADS = 2
GRID_W = 64
CONV_WIDTH = 31
VCONV_PIECE = 16
HCONV_MARGIN = 16
HCONV_PITCH = 100
TOP_K = 4
SWIGLU_ALPHA = 1.702
SWIGLU_LIMIT = 7.0

V7X_LANES = 128
V7X_VMEM_BYTES = 64 * 1024 * 1024
VMEM_HEADROOM_BYTES = 8 * 1024 * 1024


def _params(semantics, vmem_bytes):
    limit = min(int(vmem_bytes), V7X_VMEM_BYTES - VMEM_HEADROOM_BYTES)
    return pltpu.CompilerParams(dimension_semantics=semantics, vmem_limit_bytes=limit)


def _dot(a, b):
    return jnp.dot(a, b, preferred_element_type=F32)


def _dot_nt(a, b):
    return lax.dot_general(a, b, (((1,), (1,)), ((), ())), preferred_element_type=F32)


def _dot_tn(a, b):
    return lax.dot_general(a, b, (((0,), (0,)), ((), ())), preferred_element_type=F32)


def _rmsnorm_mod(x, g, scale, shift):
    y = x * lax.rsqrt(jnp.mean(x * x, axis=-1, keepdims=True) + EPS)
    return y * g * (1.0 + scale) + shift


def _store_row_slabs(ref, x, *lead):
    m, d = x.shape
    slabs = d // V7X_LANES
    for s in range(slabs):
        ref[(*lead, pl.ds(s, m, stride=slabs), slice(None))] = x[:, s * V7X_LANES:(s + 1) * V7X_LANES]


def _load_row_slabs(ref, slabs, *lead):
    m = ref.shape[-2] // slabs
    return jnp.concatenate([ref[(*lead, pl.ds(s, m, stride=slabs), slice(None))] for s in range(slabs)], axis=1)


def _slab(row, slabs):
    return pl.ds(pl.multiple_of(row * slabs, slabs), slabs)


def _ada_kernel(c_ref, w_ref, b_ref, o_ref):
    s = c_ref[...]
    s = s * jax.nn.sigmoid(s)
    o_ref[0] = jnp.dot(s, w_ref[0], preferred_element_type=F32,
                       precision=lax.Precision.HIGHEST) + b_ref[0]


def _ada_mods(cond, w_ada, b_ada):
    depth, d, d6 = w_ada.shape
    rows = cond.shape[0]
    tn = 1536
    return pl.pallas_call(
        _ada_kernel,
        grid=(depth, d6 // tn),
        in_specs=[pl.BlockSpec((rows, d), lambda l, n: (0, 0)),
                  pl.BlockSpec((1, d, tn), lambda l, n: (l, 0, n)),
                  pl.BlockSpec((1, 1, tn), lambda l, n: (l, 0, n))],
        out_specs=pl.BlockSpec((1, rows, tn), lambda l, n: (l, 0, n)),
        out_shape=jax.ShapeDtypeStruct((depth, rows, d6), F32),
        compiler_params=_params(("arbitrary", "arbitrary"), 24 << 20),
        name="ada_mods",
    )(cond, w_ada, b_ada.reshape(depth, 1, d6))


def _chunk_cumsum(x, chunk, reverse):
    rows, cols = x.shape
    pos = lax.broadcasted_iota(I32, (rows, V7X_LANES), 0) & (chunk - 1)
    step = 1
    while step < chunk:
        keep = (pos < chunk - step) if reverse else (pos >= step)
        shift = rows - step if reverse else step
        parts = []
        for c in range(cols // V7X_LANES):
            xs = x[:, c * V7X_LANES:(c + 1) * V7X_LANES]
            parts.append(xs + jnp.where(keep, pltpu.roll(xs, shift, 0), 0.0))
        x = jnp.concatenate(parts, axis=1)
        step *= 2
    return x


def _hgrn_proj_kernel(x_ref, mod_ref, gmix_ref, lb_ref, w_hbm,
                      q_ref, kf_ref, kb_ref, v_ref, bf_ref, cb_ref, g_ref, w_bf, w_stage, sem):
    d = x_ref.shape[-1]

    @pl.when(jnp.logical_and(pl.program_id(0) == 0, pl.program_id(1) == 0))
    def _():
        for cblk in range(w_bf.shape[1] // d):
            cols = pl.ds(cblk * d, d)
            fetch = pltpu.make_async_copy(w_hbm.at[:, cols], w_stage, sem)
            fetch.start()
            fetch.wait()
            w_bf[:, cols] = w_stage[...].astype(BF16)

    h = _rmsnorm_mod(x_ref[0], gmix_ref[...], mod_ref[0, 1:2, :], mod_ref[0, 0:1, :])
    p = _dot(h.astype(BF16), w_bf[...])
    q_ref[0] = p[:, 0:d].astype(BF16)
    v_ref[0] = p[:, d:2 * d].astype(BF16)
    g_ref[0] = p[:, 4 * d:5 * d].astype(BF16)
    lbf = lb_ref[0:1, :]
    lbb = lb_ref[1:2, :]
    ff = lbf + (1.0 - lbf) * jax.nn.sigmoid(p[:, 2 * d:3 * d])
    fb = lbb + (1.0 - lbb) * jax.nn.sigmoid(p[:, 3 * d:4 * d])
    kf_ref[0] = (1.0 - ff).astype(BF16)
    kb_ref[0] = (1.0 - fb).astype(BF16)
    bf_ref[0] = _chunk_cumsum(jnp.log(ff), GLA_CHUNK, reverse=False)
    cb_ref[0] = _chunk_cumsum(jnp.log(fb), GLA_CHUNK, reverse=True)


def _hgrn_proj(x, mods, g_mix, lb, w_in, tm):
    b, l, d = x.shape
    d5 = w_in.shape[1]
    tok = lambda bi, i: (bi, i, 0)
    const2 = lambda bi, i: (0, 0)
    bf_sd = jax.ShapeDtypeStruct((b, l, d), BF16)
    f_sd = jax.ShapeDtypeStruct((b, l, d), F32)
    return pl.pallas_call(
        _hgrn_proj_kernel,
        grid=(b, l // tm),
        in_specs=[pl.BlockSpec((1, tm, d), tok),
                  pl.BlockSpec((1, 6, d), lambda bi, i: (bi, 0, 0)),
                  pl.BlockSpec((1, d), const2),
                  pl.BlockSpec((2, d), const2),
                  pl.BlockSpec(memory_space=pl.ANY)],
        out_specs=[pl.BlockSpec((1, tm, d), tok)] * 7,
        out_shape=[bf_sd, bf_sd, bf_sd, bf_sd, f_sd, f_sd, bf_sd],
        scratch_shapes=[pltpu.VMEM((d, d5), BF16), pltpu.VMEM((d, d), F32), pltpu.SemaphoreType.DMA(())],
        compiler_params=_params(("arbitrary", "arbitrary"), 52 << 20),
        name="hgrn_proj",
    )(x, mods, g_mix, lb, w_in)


def _gla_reference_rows(row_at, c, reverse):
    h, qt = c // 2, c // 4
    if reverse:
        return row_at(qt), row_at(h + qt), row_at(h)
    return row_at(qt - 1), row_at(h + qt - 1), row_at(h - 1)


def _gla_quarter_decay(row_at, c, reverse):
    h = c // 2
    m_lo, m_hi, _ = _gla_reference_rows(row_at, c, reverse)
    return jnp.maximum(jnp.maximum(jnp.abs(row_at(0) - m_lo), jnp.abs(row_at(h - 1) - m_lo)),
                       jnp.maximum(jnp.abs(row_at(h) - m_hi), jnp.abs(row_at(c - 1) - m_hi)))


def _gla_chunk(q, k, v, cum, st, *, reverse, mode):
    c = q.shape[0]
    qf = q.astype(F32)
    kf = k.astype(F32)
    last = cum[0:1] if reverse else cum[c - 1:c]
    row = lax.broadcasted_iota(I32, (c, c), 0)
    col = lax.broadcasted_iota(I32, (c, c), 1)
    if mode == "one_reference":
        mid_row = c // 2 if reverse else c // 2 - 1
        mid = cum[mid_row:mid_row + 1]
        x = cum - mid
        qe = qf * jnp.exp(x)
        ke = kf * jnp.exp(-x)
        a = _dot_nt(qe.astype(BF16), ke.astype(BF16))
        a = jnp.where((col >= row) if reverse else (row >= col), a, 0.0)
        qh = (qe * jnp.exp(mid)).astype(BF16)
        kh = (ke * jnp.exp(last - mid)).astype(BF16)
    elif mode == "three_references":
        m_lo, m_hi, bnd = _gla_reference_rows(lambda r: cum[r:r + 1], c, reverse)
        first = lax.broadcasted_iota(I32, cum.shape, 0) < c // 2
        x_own = cum - jnp.where(first, m_lo, m_hi)
        x_bnd = cum - bnd
        q_own = qf * jnp.exp(x_own)
        k_own = kf * jnp.exp(-x_own)
        q_bnd = qf * jnp.exp(x_bnd)
        k_bnd = kf * jnp.exp(-x_bnd)
        q_cross = jnp.where(first, q_bnd, 0.0) if reverse else jnp.where(first, 0.0, q_bnd)
        k_cross = jnp.where(first, 0.0, k_bnd) if reverse else jnp.where(first, k_bnd, 0.0)
        qe = jnp.concatenate([jnp.where(first, q_own, 0.0), jnp.where(first, 0.0, q_own), q_cross], axis=1)
        ke = jnp.concatenate([jnp.where(first, k_own, 0.0), jnp.where(first, 0.0, k_own), k_cross], axis=1)
        a = _dot_nt(qe.astype(BF16), ke.astype(BF16))
        a = jnp.where((col >= row) if reverse else (row >= col), a, 0.0)
        qh = (qf * jnp.exp(cum)).astype(BF16)
        kh = (kf * jnp.exp(last - cum)).astype(BF16)
    else:
        a = jnp.where(row == col, _dot_nt(q, k), 0.0)
        c1 = cum.astype(BF16)
        r1 = cum - c1.astype(F32)
        c2 = r1.astype(BF16)
        c3 = (r1 - c2.astype(F32)).astype(BF16)
        level = 0
        while (1 << level) < c:
            m = 1 << level
            seg_row = row >> (level + 1)
            ref_row = (seg_row << (level + 1)) + (m if reverse else m - 1)
            sel = jnp.where(col == ref_row, 1.0, 0.0).astype(BF16)
            cref = _dot(sel, c1) + _dot(sel, c2) + _dot(sel, c3)
            e = jnp.exp(-jnp.abs(cum - cref))
            contrib = _dot_nt((qf * e).astype(BF16), (kf * e).astype(BF16))
            row_bit = (row >> level) & 1
            col_bit = (col >> level) & 1
            same = seg_row == (col >> (level + 1))
            if reverse:
                pick = jnp.where(same, (1 - row_bit) * col_bit, 0)
            else:
                pick = jnp.where(same, row_bit * (1 - col_bit), 0)
            a = a + jnp.where(pick == 1, contrib, 0.0)
            level += 1
        qh = (qf * jnp.exp(cum)).astype(BF16)
        kh = (kf * jnp.exp(last - cum)).astype(BF16)
    o = _dot(a.astype(BF16), v) + _dot_nt(qh, st.astype(BF16))
    st_new = st * jnp.exp(last) + _dot_tn(v, kh)
    return o, st_new


def _gla_scan_kernel(qf_ref, kf_ref, vf_ref, bf_ref, qb_ref, kb_ref, vb_ref, cb_ref, s0f_ref, s0b_ref,
                     of_ref, ob_ref, sf_ref, sb_ref):
    j = pl.program_id(2)

    @pl.when(j == 0)
    def _():
        sf_ref[...] = s0f_ref[...]
        sb_ref[...] = s0b_ref[...]

    t = qf_ref.shape[1]
    n_chunks = t // GLA_CHUNK
    n_heads = qf_ref.shape[2] // HEAD_DIM

    worst_half = jnp.zeros((1, qf_ref.shape[2]), F32)
    worst_quarter = jnp.zeros((1, qf_ref.shape[2]), F32)
    for c in range(n_chunks):
        lo = c * GLA_CHUNK
        for cum_ref, reverse in ((bf_ref, False), (cb_ref, True)):
            row_at = lambda r, cum_ref=cum_ref, lo=lo: cum_ref[0, lo + r:lo + r + 1, :]
            mid = row_at(GLA_CHUNK // 2 if reverse else GLA_CHUNK // 2 - 1)
            worst_half = jnp.maximum(worst_half, jnp.maximum(jnp.abs(row_at(0) - mid),
                                                             jnp.abs(row_at(GLA_CHUNK - 1) - mid)))
            worst_quarter = jnp.maximum(worst_quarter, _gla_quarter_decay(row_at, GLA_CHUNK, reverse))
    mild = jnp.max(worst_half) <= GLA_HALF_DECAY_LIMIT
    medium = jnp.logical_and(jnp.logical_not(mild), jnp.max(worst_quarter) <= GLA_DECAY_LIMIT)
    steep = jnp.logical_not(jnp.logical_or(mild, medium))

    directions = ((qf_ref, kf_ref, vf_ref, bf_ref, of_ref, sf_ref, False),
                  (qb_ref, kb_ref, vb_ref, cb_ref, ob_ref, sb_ref, True))

    def chunk_step(refs, h, c, st, mode):
        q_ref, k_ref, v_ref, cum_ref, o_ref, _, reverse = refs
        ls = slice(h * HEAD_DIM, (h + 1) * HEAD_DIM)
        start = c * GLA_CHUNK
        rs = pl.ds(start if isinstance(c, int) else pl.multiple_of(start, GLA_CHUNK), GLA_CHUNK)
        o, st = _gla_chunk(q_ref[0, rs, ls], k_ref[0, rs, ls], v_ref[0, rs, ls], cum_ref[0, rs, ls],
                           st, reverse=reverse, mode=mode)
        o_ref[0, rs, ls] = o
        return st

    def run_looped(mode):
        for refs in directions:
            for h in range(n_heads):
                def body(i, st, refs=refs, h=h):
                    c = n_chunks - 1 - i if refs[6] else i
                    return chunk_step(refs, h, c, st, mode)

                refs[5][0, h] = lax.fori_loop(0, n_chunks, body, refs[5][0, h])

    @pl.when(steep)
    def _():
        run_looped("dyadic")

    @pl.when(medium)
    def _():
        run_looped("three_references")

    @pl.when(mild)
    def _():
        for h in range(n_heads):
            for refs in directions:
                st = refs[5][0, h]
                for i in range(n_chunks):
                    st = chunk_step(refs, h, n_chunks - 1 - i if refs[6] else i, st, "one_reference")
                refs[5][0, h] = st


def _gla_scan(q, kf, kb, v, bf, cb, s0f, s0b, t, heads_per_step):
    b, l, d = q.shape
    lanes = heads_per_step * HEAD_DIM
    nblk = l // t
    fwd = lambda bi, hp, j: (bi, j, hp)
    bwd = lambda bi, hp, j: (bi, nblk - 1 - j, hp)
    st_map = lambda bi, hp, j: (bi, hp, 0, 0)
    tok_f = pl.BlockSpec((1, t, lanes), fwd)
    tok_b = pl.BlockSpec((1, t, lanes), bwd)
    st_spec = pl.BlockSpec((1, heads_per_step, HEAD_DIM, HEAD_DIM), st_map)
    o_sd = jax.ShapeDtypeStruct((b, l, d), F32)
    s_sd = jax.ShapeDtypeStruct(s0f.shape, F32)
    return pl.pallas_call(
        _gla_scan_kernel,
        grid=(b, d // lanes, nblk),
        in_specs=[tok_f, tok_f, tok_f, tok_f, tok_b, tok_b, tok_b, tok_b, st_spec, st_spec],
        out_specs=[tok_f, tok_b, st_spec, st_spec],
        out_shape=[o_sd, o_sd, s_sd, s_sd],
        compiler_params=_params(("arbitrary", "arbitrary", "arbitrary"), 32 << 20),
        name="gla_scan",
    )(q, kf, v, bf, q, kb, v, cb, s0f, s0b)


def _ffn_route_tail(x, y, mod_ref, gffn_ref, wr_ref, br_ref, x1_ref, xs_ref, route_ref, gate_ref, cnt_ref,
                    stage, dest_vm, dest_sm, cnt_vm, cnt_sm, zero_buf, sem, *, expert_stride, fill_rows):
    tm, d = x.shape
    slabs = d // V7X_LANES
    i = pl.program_id(0)
    par = lax.rem(i, 2)
    x1 = x + mod_ref[0, 2:3, :] * y
    x1_ref[...] = x1
    hf = _rmsnorm_mod(x1, gffn_ref[...], mod_ref[0, 4:5, :], mod_ref[0, 3:4, :])
    _store_row_slabs(stage, hf, par)

    logits = _dot_nt(wr_ref[...], hf.astype(BF16)) + br_ref[:, 0:1]
    n_exp = logits.shape[0]
    eio = lax.broadcasted_iota(I32, (n_exp, tm), 0)
    vals, idxs, hots = [], [], []
    rest = logits
    for _ in range(TOP_K):
        m = jnp.max(rest, axis=0, keepdims=True)
        idx = jnp.min(jnp.where(rest == m, eio, n_exp), axis=0, keepdims=True)
        hot = eio == idx
        rest = jnp.where(hot, -jnp.inf, rest)
        vals.append(m)
        idxs.append(idx)
        hots.append(hot)
    ex = [jnp.exp(vk - vals[0]) for vk in vals]
    den = ex[0] + ex[1] + ex[2] + ex[3]
    gates = [e / den for e in ex]

    picked = jnp.zeros((n_exp, tm), F32)
    for hot in hots:
        picked = picked + jnp.where(hot, 1.0, 0.0)
    earlier = lax.broadcasted_iota(I32, (tm, tm), 0) < lax.broadcasted_iota(I32, (tm, tm), 1)
    before = _dot(picked.astype(BF16), jnp.where(earlier, 1.0, 0.0).astype(BF16)) + cnt_ref[:, 0:1]
    ranks = [jnp.sum(jnp.where(hot, before, 0.0), axis=0, keepdims=True).astype(I32) for hot in hots]
    cnt_ref[...] = cnt_ref[...] + jnp.sum(picked, axis=1, keepdims=True)

    route_ref[...] = jnp.concatenate(idxs + ranks, axis=0)
    gate_ref[...] = jnp.concatenate(gates + [jnp.zeros_like(g) for g in gates], axis=0)

    dests = [(idx * expert_stride + rank) * slabs for idx, rank in zip(idxs, ranks)]
    dest_vm[...] = jnp.concatenate(dests + [jnp.zeros_like(dd) for dd in dests], axis=0)
    to_smem = pltpu.make_async_copy(dest_vm, dest_sm, sem.at[2])
    to_smem.start()
    to_smem.wait()

    def issue(r, carry):
        for k in range(TOP_K):
            pltpu.make_async_copy(stage.at[par, _slab(r, slabs)],
                                  xs_ref.at[pl.ds(pl.multiple_of(dest_sm[k, r], slabs), slabs)],
                                  sem.at[par]).start(priority=k % DMA_THREADS)
        return carry

    lax.fori_loop(0, tm, issue, 0, unroll=DMA_ISSUE_UNROLL)

    def wait_rows(parity):
        for k in range(TOP_K):
            pltpu.make_async_copy(stage.at[0], xs_ref.at[pl.ds(0, tm * slabs)], sem.at[parity]).wait()

    @pl.when(i > 0)
    def _():
        wait_rows(1 - par)

    @pl.when(i == pl.num_programs(0) - 1)
    def _():
        wait_rows(par)
        zero_buf[...] = jnp.zeros_like(zero_buf)
        cnt_vm[...] = cnt_ref[...].astype(I32)
        cnt_copy = pltpu.make_async_copy(cnt_vm, cnt_sm, sem.at[2])
        cnt_copy.start()
        cnt_copy.wait()

        def pad_rows(e):
            first = e * expert_stride + cnt_sm[e, 0]
            return xs_ref.at[pl.ds(pl.multiple_of(first * slabs, slabs), fill_rows * slabs)]

        for e in range(cnt_sm.shape[0]):
            pltpu.make_async_copy(zero_buf, pad_rows(e), sem.at[2]).start()
        for e in range(cnt_sm.shape[0]):
            pltpu.make_async_copy(zero_buf, pad_rows(e), sem.at[2]).wait()


def _hgrn_out_kernel(of_ref, ob_ref, g_ref, x_ref, mod_ref, gout_ref, wout_ref, gffn_ref, wr_ref, br_ref,
                     x1_ref, xs_ref, route_ref, gate_ref, cnt_ref, *scratch, **dispatch):
    @pl.when(pl.program_id(0) == 0)
    def _():
        cnt_ref[...] = jnp.zeros_like(cnt_ref)

    o = of_ref[...] + ob_ref[...]
    parts = []
    for h in range(o.shape[1] // HEAD_DIM):
        oh = o[:, h * HEAD_DIM:(h + 1) * HEAD_DIM]
        parts.append(oh * lax.rsqrt(jnp.mean(oh * oh, axis=-1, keepdims=True) + EPS))
    on = jnp.concatenate(parts, axis=1) * gout_ref[...]
    g = g_ref[...].astype(F32)
    a = on * (g * jax.nn.sigmoid(g))
    y = _dot(a.astype(BF16), wout_ref[...])
    _ffn_route_tail(x_ref[...], y, mod_ref, gffn_ref, wr_ref, br_ref,
                    x1_ref, xs_ref, route_ref, gate_ref, cnt_ref, *scratch, **dispatch)


def _conv_out_kernel(a_ref, x_ref, mod_ref, wout_ref, bout_ref, gffn_ref, wr_ref, br_ref,
                     x1_ref, xs_ref, route_ref, gate_ref, cnt_ref, *scratch, **dispatch):
    @pl.when(pl.program_id(0) == 0)
    def _():
        cnt_ref[...] = jnp.zeros_like(cnt_ref)

    y = _dot(a_ref[...], wout_ref[...]) + bout_ref[...]
    _ffn_route_tail(x_ref[...], y, mod_ref, gffn_ref, wr_ref, br_ref,
                    x1_ref, xs_ref, route_ref, gate_ref, cnt_ref, *scratch, **dispatch)


def _tail_specs(n, d, n_exp, tm, tiles_per_batch, fill_rows):
    tok = lambda i: (i, 0)
    const2 = lambda i: (0, 0)
    slabs = d // V7X_LANES
    expert_stride = n + fill_rows
    specs = dict(
        tok=pl.BlockSpec((tm, d), tok),
        mod=pl.BlockSpec((1, 6, d), lambda i: (i // tiles_per_batch, 0, 0)),
        vec=pl.BlockSpec((1, d), const2),
        mat=pl.BlockSpec((d, d), const2),
        wr=pl.BlockSpec((n_exp, d), const2),
        br=pl.BlockSpec((n_exp, V7X_LANES), const2),
    )
    out_specs = [pl.BlockSpec((tm, d), tok),
                 pl.BlockSpec(memory_space=pl.ANY),
                 pl.BlockSpec((8, tm), lambda i: (0, i)),
                 pl.BlockSpec((8, tm), lambda i: (0, i)),
                 pl.BlockSpec((n_exp, V7X_LANES), const2)]
    out_shape = [jax.ShapeDtypeStruct((n, d), F32),
                 jax.ShapeDtypeStruct((n_exp * expert_stride * slabs, V7X_LANES), F32),
                 jax.ShapeDtypeStruct((8, n), I32),
                 jax.ShapeDtypeStruct((8, n), F32),
                 jax.ShapeDtypeStruct((n_exp, V7X_LANES), F32)]
    scratch = [pltpu.VMEM((2, tm * slabs, V7X_LANES), F32),
               pltpu.VMEM((8, tm), I32), pltpu.SMEM((8, tm), I32),
               pltpu.VMEM((n_exp, V7X_LANES), I32), pltpu.SMEM((n_exp, V7X_LANES), I32),
               pltpu.VMEM((fill_rows * slabs, V7X_LANES), F32),
               pltpu.SemaphoreType.DMA((3,))]
    dispatch = dict(expert_stride=expert_stride, fill_rows=fill_rows)
    return specs, out_specs, out_shape, scratch, dispatch


def _hgrn_out(o_f, o_b, g, x, mods, g_out, w_out, g_ffn, wr_t, br, tm, fill_rows):
    n, d = x.shape
    s, out_specs, out_shape, scratch, dispatch = _tail_specs(
        n, d, wr_t.shape[0], tm, n // mods.shape[0] // tm, fill_rows)
    return pl.pallas_call(
        functools.partial(_hgrn_out_kernel, **dispatch),
        grid=(n // tm,),
        in_specs=[s["tok"], s["tok"], s["tok"], s["tok"], s["mod"], s["vec"], s["mat"], s["vec"], s["wr"], s["br"]],
        out_specs=out_specs,
        out_shape=out_shape,
        scratch_shapes=scratch,
        compiler_params=_params(("arbitrary",), 52 << 20),
        name="hgrn_out_route",
    )(o_f, o_b, g, x, mods, g_out, w_out, g_ffn, wr_t, br)


def _conv_out(a, x, mods, w_out, b_out, g_ffn, wr_t, br, tm, fill_rows):
    n, d = x.shape
    s, out_specs, out_shape, scratch, dispatch = _tail_specs(
        n, d, wr_t.shape[0], tm, n // mods.shape[0] // tm, fill_rows)
    return pl.pallas_call(
        functools.partial(_conv_out_kernel, **dispatch),
        grid=(n // tm,),
        in_specs=[s["tok"], s["tok"], s["mod"], s["mat"], s["vec"], s["vec"], s["wr"], s["br"]],
        out_specs=out_specs,
        out_shape=out_shape,
        scratch_shapes=scratch,
        compiler_params=_params(("arbitrary",), 44 << 20),
        name="conv_out_route",
    )(a, x, mods, w_out, b_out, g_ffn, wr_t, br)


def _routing_plan(route, counts, n_blocks, tmb, tm, expert_stride, slabs):
    n_exp = counts.shape[0]
    padded = (counts + tmb - 1) // tmb * tmb
    pend = jnp.cumsum(padded)
    pstart = pend - padded
    n_valid = pend[-1] // tmb
    blk = jnp.minimum(jnp.arange(n_blocks, dtype=I32), n_valid - 1)
    blk_e = jnp.minimum(jnp.sum((blk[:, None] * tmb >= pend[None, :]).astype(I32), axis=1), n_exp - 1)
    blk_in_e = blk - jnp.sum(jnp.where(blk_e[:, None] == jnp.arange(n_exp, dtype=I32)[None, :],
                                       (pstart // tmb)[None, :], 0), axis=1)
    blk_row = blk_e * (expert_stride // tmb) + blk_in_e
    eidx, rank = route[:TOP_K], route[TOP_K:2 * TOP_K]
    hit = eidx[None] == jnp.arange(n_exp, dtype=I32)[:, None, None]
    dest = (rank + jnp.sum(jnp.where(hit, pstart[:, None, None], 0), axis=0)) * slabs
    n = dest.shape[1]
    dest_tiles = dest.reshape(TOP_K, n // tm, tm).transpose(1, 0, 2).reshape(-1)
    return dest_tiles.astype(I32), blk_e.astype(I32), blk_row.astype(I32), n_valid.reshape(1).astype(I32)


def _expert_kernel(blk_e_ref, blk_row_ref, n_valid_ref, xs_ref, w1_ref, b1_ref, w2_ref, b2_ref, ys_ref, w1_bf, w2_bf):
    del blk_row_ref
    i = pl.program_id(0)
    live = i < n_valid_ref[0]
    new_expert = jnp.logical_or(i == 0, blk_e_ref[i] != blk_e_ref[jnp.maximum(i - 1, 0)])

    @pl.when(jnp.logical_and(live, new_expert))
    def _():
        w1_bf[...] = w1_ref[0, 0].astype(BF16)
        w2_bf[...] = w2_ref[0, 0].astype(BF16)

    @pl.when(live)
    def _():
        slabs = w1_bf.shape[0] // V7X_LANES
        h = _dot(_load_row_slabs(xs_ref, slabs).astype(BF16), w1_bf[...]) + b1_ref[0, 0]
        de = h.shape[1] // 2
        gate = jnp.minimum(h[:, :de], SWIGLU_LIMIT)
        up = jnp.clip(h[:, de:], -SWIGLU_LIMIT, SWIGLU_LIMIT)
        act = (up + 1.0) * gate * jax.nn.sigmoid(SWIGLU_ALPHA * gate)
        _store_row_slabs(ys_ref, _dot(act.astype(BF16), w2_bf[...]) + b2_ref[0, 0])

    @pl.when(jnp.logical_not(live))
    def _():
        ys_ref[...] = jnp.zeros_like(ys_ref)


def _experts(blk_e, blk_row, n_valid, xs, layer, w1, b1, w2, b2, n_blocks, tmb):
    _, n_exp, d, de2 = w1.shape
    slabs = d // V7X_LANES
    row_map = lambda i, be, br, nv: (br[i], 0)
    w_map = lambda i, be, br, nv: (layer, be[i], 0, 0)
    return pl.pallas_call(
        _expert_kernel,
        grid_spec=pltpu.PrefetchScalarGridSpec(
            num_scalar_prefetch=3,
            grid=(n_blocks,),
            in_specs=[pl.BlockSpec((tmb * slabs, V7X_LANES), row_map),
                      pl.BlockSpec((1, 1, d, de2), w_map),
                      pl.BlockSpec((1, 1, 1, de2), w_map),
                      pl.BlockSpec((1, 1, de2 // 2, d), w_map),
                      pl.BlockSpec((1, 1, 1, d), w_map)],
            out_specs=pl.BlockSpec((tmb * slabs, V7X_LANES), lambda i, be, br, nv: (i, 0)),
            scratch_shapes=[pltpu.VMEM((d, de2), BF16), pltpu.VMEM((de2 // 2, d), BF16)]),
        out_shape=jax.ShapeDtypeStruct((n_blocks * tmb * slabs, V7X_LANES), xs.dtype),
        compiler_params=_params(("arbitrary",), 56 << 20),
        name="moe_experts",
    )(blk_e, blk_row, n_valid, xs, w1, b1.reshape(-1, n_exp, 1, de2), w2, b2.reshape(-1, n_exp, 1, d))


def _gather_start(dest_ref, ys_ref, buf, sem, slot, tm, slabs):
    def issue(r, carry):
        for k in range(TOP_K):
            pltpu.make_async_copy(ys_ref.at[pl.ds(pl.multiple_of(dest_ref[k * tm + r], slabs), slabs)],
                                  buf.at[slot, k, _slab(r, slabs)], sem.at[slot]).start(priority=k % DMA_THREADS)
        return carry

    lax.fori_loop(0, tm, issue, 0, unroll=DMA_ISSUE_UNROLL)


def _gather_combine(dest_ref, dest_next_ref, ys_ref, gate_ref, buf, sem):
    i = pl.program_id(0)
    slot = lax.rem(i, 2)
    tm = gate_ref.shape[0]
    slabs = buf.shape[2] // tm

    @pl.when(i == 0)
    def _():
        _gather_start(dest_ref, ys_ref, buf, sem, 0, tm, slabs)

    @pl.when(i + 1 < pl.num_programs(0))
    def _():
        _gather_start(dest_next_ref, ys_ref, buf, sem, 1 - slot, tm, slabs)

    for k in range(TOP_K):
        pltpu.make_async_copy(ys_ref.at[pl.ds(0, tm * slabs)], buf.at[slot, k], sem.at[slot]).wait()
    y = None
    for k in range(TOP_K):
        rows = _load_row_slabs(buf, slabs, slot, k)
        y = gate_ref[:, k:k + 1] * rows if y is None else y + gate_ref[:, k:k + 1] * rows
    return y


def _combine_conv_in_kernel(dest_ref, dest_next_ref, ys_ref, gate_ref, x_ref, mod_ref, modn_ref, gmix_ref,
                            win_ref, bin_ref, x2_ref, u_ref, buf, sem):
    y = _gather_combine(dest_ref, dest_next_ref, ys_ref, gate_ref, buf, sem)
    x2 = x_ref[...] + mod_ref[0, 5:6, :] * y
    x2_ref[...] = x2
    h = _rmsnorm_mod(x2, gmix_ref[...], modn_ref[0, 1:2, :], modn_ref[0, 0:1, :])
    p = _dot(h.astype(BF16), win_ref[...]) + bin_ref[...]
    d = x2.shape[1]
    u_ref[...] = p[:, :d] * jax.nn.sigmoid(p[:, d:])


def _combine_final_kernel(dest_ref, dest_next_ref, ys_ref, gate_ref, x_ref, mod_ref, gfin_ref, out_ref, buf, sem):
    y = _gather_combine(dest_ref, dest_next_ref, ys_ref, gate_ref, buf, sem)
    x4 = x_ref[...] + mod_ref[0, 5:6, :] * y
    out_ref[...] = x4 * lax.rsqrt(jnp.mean(x4 * x4, axis=-1, keepdims=True) + EPS) * gfin_ref[...]


def _combine_common(n, d, tm, tiles_per_batch):
    tok = lambda i: (i, 0)
    last = n // tm - 1
    return dict(
        dest=pl.BlockSpec((TOP_K * tm,), lambda i: (i,), memory_space=pltpu.SMEM),
        dest_next=pl.BlockSpec((TOP_K * tm,), lambda i: (jnp.minimum(i + 1, last),), memory_space=pltpu.SMEM),
        ys=pl.BlockSpec(memory_space=pl.ANY),
        gate=pl.BlockSpec((tm, 8), tok),
        tok=pl.BlockSpec((tm, d), tok),
        mod=pl.BlockSpec((1, 6, d), lambda i: (i // tiles_per_batch, 0, 0)),
        vec=pl.BlockSpec((1, d), lambda i: (0, 0)),
        scratch=[pltpu.VMEM((2, TOP_K, tm * (d // V7X_LANES), V7X_LANES), F32), pltpu.SemaphoreType.DMA((2,))],
    )


def _combine_conv_in(dest_tiles, ys, gates, x, mods, mods_next, g_mix, w_in, b_in, tm):
    n, d = x.shape
    s = _combine_common(n, d, tm, n // mods.shape[0] // tm)
    return pl.pallas_call(
        _combine_conv_in_kernel,
        grid=(n // tm,),
        in_specs=[s["dest"], s["dest_next"], s["ys"], s["gate"], s["tok"], s["mod"], s["mod"], s["vec"],
                  pl.BlockSpec((d, 2 * d), lambda i: (0, 0)),
                  pl.BlockSpec((1, 2 * d), lambda i: (0, 0))],
        out_specs=[s["tok"], s["tok"]],
        out_shape=[jax.ShapeDtypeStruct((n, d), F32), jax.ShapeDtypeStruct((n, d), F32)],
        scratch_shapes=s["scratch"],
        compiler_params=_params(("arbitrary",), 48 << 20),
        name="moe_combine_conv_in",
    )(dest_tiles, dest_tiles, ys, gates, x, mods, mods_next, g_mix, w_in, b_in)


def _combine_final(dest_tiles, ys, gates, x, mods, g_final, tm):
    n, d = x.shape
    s = _combine_common(n, d, tm, n // mods.shape[0] // tm)
    return pl.pallas_call(
        _combine_final_kernel,
        grid=(n // tm,),
        in_specs=[s["dest"], s["dest_next"], s["ys"], s["gate"], s["tok"], s["mod"], s["vec"]],
        out_specs=s["tok"],
        out_shape=jax.ShapeDtypeStruct((n, d), F32),
        scratch_shapes=s["scratch"],
        compiler_params=_params(("arbitrary",), 32 << 20),
        name="moe_combine_final",
    )(dest_tiles, dest_tiles, ys, gates, x, mods, g_final)


def _moe_experts(route, counts, xs, layer, w1, b1, w2, b2, tmb, tm):
    n = route.shape[1]
    n_exp = w1.shape[1]
    slabs = w1.shape[2] // V7X_LANES
    expert_stride = xs.shape[0] // (n_exp * slabs)
    n_blocks = (n * TOP_K + n_exp * (tmb - 1) + tmb - 1) // tmb
    dest_tiles, blk_e, blk_row, n_valid = _routing_plan(route, counts, n_blocks, tmb, tm, expert_stride, slabs)
    ys = _experts(blk_e, blk_row, n_valid, xs, layer, w1, b1, w2, b2, n_blocks, tmb)
    return dest_tiles, ys


def _dwconv_kernel(cur_ref, prev_ref, next_ref, w_ref, bdw_ref, gln_ref, bln_ref, out_ref, hbuf, vbuf, zbuf):
    i = pl.program_id(1)
    rt, w, d = cur_ref.shape[1], cur_ref.shape[2], cur_ref.shape[3]
    half = d // 2
    pad = CONV_WIDTH // 2
    n_lg = half // V7X_LANES

    for r in range(rt):
        base = r * HCONV_PITCH
        for lg in range(n_lg):
            hbuf[lg, base:base + HCONV_MARGIN, :] = jnp.zeros((HCONV_MARGIN, V7X_LANES), F32)
            hbuf[lg, base + HCONV_MARGIN:base + HCONV_MARGIN + w, :] = cur_ref[0, r, :, lg * V7X_LANES:(lg + 1) * V7X_LANES]
            hbuf[lg, base + HCONV_MARGIN + w:base + HCONV_PITCH, :] = jnp.zeros(
                (HCONV_PITCH - HCONV_MARGIN - w, V7X_LANES), F32)
    vbuf[0:rt] = jnp.where(i > 0, prev_ref[0], 0.0)
    vbuf[rt:2 * rt] = cur_ref[0, :, :, half:d]
    vbuf[2 * rt:3 * rt] = jnp.where(i < pl.num_programs(1) - 1, next_ref[0], 0.0)

    def col_body(wi, carry):
        for lg in range(n_lg):
            ls = slice(lg * V7X_LANES, (lg + 1) * V7X_LANES)
            acc = jnp.zeros((rt, V7X_LANES), F32)
            for j in range(CONV_WIDTH):
                taps = hbuf[lg, pl.ds(wi + (HCONV_MARGIN - pad + j), rt, stride=HCONV_PITCH), :]
                acc = acc + taps * w_ref[j:j + 1, ls]
            zbuf[lg, pl.ds(wi, rt, stride=w), :] = acc
        return carry

    lax.fori_loop(0, w, col_body, 0)

    def row_body(r, carry):
        def piece_body(p, carry_p):
            w0 = pl.multiple_of(p * VCONV_PIECE, VCONV_PIECE)
            for lg in range(n_lg):
                ls = slice(lg * V7X_LANES, (lg + 1) * V7X_LANES)
                ls_hi = slice(half + lg * V7X_LANES, half + (lg + 1) * V7X_LANES)
                acc = jnp.zeros((VCONV_PIECE, V7X_LANES), F32)
                for j in range(CONV_WIDTH):
                    acc = acc + vbuf[rt - pad + j + r, pl.ds(w0, VCONV_PIECE), ls] * w_ref[j:j + 1, ls_hi]
                zbuf[n_lg + lg, pl.ds(pl.multiple_of(r * w + w0, VCONV_PIECE), VCONV_PIECE), :] = acc
            return carry_p

        return lax.fori_loop(0, w // VCONV_PIECE, piece_body, carry)

    lax.fori_loop(0, rt, row_body, 0)

    z = jnp.concatenate([zbuf[s] for s in range(d // V7X_LANES)], axis=1) + bdw_ref[...]
    mu = jnp.mean(z, axis=-1, keepdims=True)
    zc = z - mu
    y = zc * lax.rsqrt(jnp.mean(zc * zc, axis=-1, keepdims=True) + EPS) * gln_ref[...] + bln_ref[...]
    out_ref[0] = (y * jax.nn.sigmoid(y)).astype(BF16).reshape(rt, w, d)


def _dwconv(u, w_dw, b_dw, g_ln, b_ln, rt):
    b, r, w, d = u.shape
    half = d // 2
    nb = r // rt
    vec = pl.BlockSpec((1, d), lambda bi, i: (0, 0))
    return pl.pallas_call(
        _dwconv_kernel,
        grid=(b, nb),
        in_specs=[pl.BlockSpec((1, rt, w, d), lambda bi, i: (bi, i, 0, 0)),
                  pl.BlockSpec((1, rt, w, half), lambda bi, i: (bi, jnp.maximum(i - 1, 0), 0, 1)),
                  pl.BlockSpec((1, rt, w, half), lambda bi, i: (bi, jnp.minimum(i + 1, nb - 1), 0, 1)),
                  pl.BlockSpec((CONV_WIDTH, d), lambda bi, i: (0, 0)),
                  vec, vec, vec],
        out_specs=pl.BlockSpec((1, rt, w, d), lambda bi, i: (bi, i, 0, 0)),
        out_shape=jax.ShapeDtypeStruct((b, r, w, d), BF16),
        scratch_shapes=[pltpu.VMEM((half // V7X_LANES, rt * HCONV_PITCH, V7X_LANES), F32),
                        pltpu.VMEM((3 * rt, w, half), F32),
                        pltpu.VMEM((d // V7X_LANES, rt * w, V7X_LANES), F32)],
        compiler_params=_params(("arbitrary", "arbitrary"), 48 << 20),
        name="dwconv_ln_silu",
    )(u, u, u, w_dw, b_dw, g_ln, b_ln)


def kernel(x, c, ctx, c_ctx, w_ada, b_ada, g_mix, g_ffn, w_hgrn_in, hgrn_gamma, g_hgrn_out, w_hgrn_out,
           w_cv_in, b_cv_in, w_cv_dw, b_cv_dw, g_cv_ln, b_cv_ln, w_cv_out, b_cv_out,
           w_router, b_router, w_exp_in, b_exp_in, w_exp_out, b_exp_out, g_final):
    bsz, seq, d = x.shape
    n = bsz * seq
    n_heads = d // HEAD_DIM
    n_exp = w_router.shape[-1]
    assert bsz + 1 <= 8 and seq % 512 == 0 and ctx.shape[1] % 256 == 0

    cond = jnp.zeros((8, d), F32).at[:bsz].set(c).at[bsz].set(c_ctx)
    mods = _ada_mods(cond, w_ada, b_ada).reshape(w_ada.shape[0], 8, 6, d)
    mods_lat = [mods[l, :bsz] for l in range(w_ada.shape[0])]
    mods_ctx = jnp.broadcast_to(mods[0, bsz:bsz + 1], (bsz, 6, d))

    lb_all = jnp.cumsum(jax.nn.softmax(hgrn_gamma.astype(F32), axis=1), axis=1)
    lb = lb_all[:, 0, :]

    row = lambda v: v.reshape(1, -1)
    wr_t = [w_router[l].T.astype(BF16) for l in range(2)]
    br = [jnp.broadcast_to(b_router[l][:, None], (n_exp, V7X_LANES)) for l in range(2)]

    w_in = w_hgrn_in[0]
    scan_t = 256
    zeros_state = jnp.zeros((bsz, n_heads, HEAD_DIM, HEAD_DIM), F32)
    pc = _hgrn_proj(ctx, mods_ctx, row(g_mix[0]), lb, w_in, 256)
    _, _, s_f, s_b = _gla_scan(pc[0], pc[1], pc[2], pc[3], pc[4], pc[5], zeros_state, zeros_state, scan_t, n_heads)
    pq, pkf, pkb, pv, pbf, pcb, pg = _hgrn_proj(x, mods_lat[0], row(g_mix[0]), lb, w_in, 256)
    o_f, o_b, _, _ = _gla_scan(pq, pkf, pkb, pv, pbf, pcb, s_f, s_b, scan_t, n_heads)

    tm_tail, tmb, tm_rows = 512, 512, 256
    x1, xs, route, gates, cnt = _hgrn_out(
        o_f.reshape(n, d), o_b.reshape(n, d), pg.reshape(n, d), x.reshape(n, d), mods_lat[0],
        row(g_hgrn_out[0]), w_hgrn_out[0].astype(BF16), row(g_ffn[0]), wr_t[0], br[0], tm_tail, tmb)
    dest_tiles, ys = _moe_experts(route, cnt[:, 0].astype(I32), xs, 0,
                                  w_exp_in, b_exp_in, w_exp_out, b_exp_out, tmb, tm_rows)

    x2, u = _combine_conv_in(dest_tiles, ys, gates.T, x1, mods_lat[0], mods_lat[1], row(g_mix[1]),
                             w_cv_in[0].astype(BF16), row(b_cv_in[0]), tm_rows)
    za = _dwconv(u.reshape(bsz, seq // GRID_W, GRID_W, d), w_cv_dw[0], row(b_cv_dw[0]),
                 row(g_cv_ln[0]), row(b_cv_ln[0]), 16)
    x3, xs, route, gates, cnt = _conv_out(
        za.reshape(n, d), x2, mods_lat[1], w_cv_out[0].astype(BF16), row(b_cv_out[0]),
        row(g_ffn[1]), wr_t[1], br[1], tm_tail, tmb)
    dest_tiles, ys = _moe_experts(route, cnt[:, 0].astype(I32), xs, 1,
                                  w_exp_in, b_exp_in, w_exp_out, b_exp_out, tmb, tm_rows)
    out = _combine_final(dest_tiles, ys, gates.T, x3, mods_lat[1], row(g_final), tm_rows)
    return out.reshape(bsz, seq, d)
```
